```python
import jax, jax.numpy as jnp
from jax import lax
import numpy as np

D_MODEL = 4096
BATCH = 2
SEQ = 8192
DEPTH = 4

MIX_WIDTH = D_MODEL
RET_WIDTH = MIX_WIDTH // 2
POOL_WIDTH = MIX_WIDTH - RET_WIDTH
RET_HEADS = 8
RET_V_DIM = RET_WIDTH // RET_HEADS
RET_QK_DIM = RET_V_DIM // 2
RET_QK_WIDTH = RET_HEADS * RET_QK_DIM
CHUNK = 128
POOL_WINDOWS = (2, 4, 8, 16)
POOL_GROUPS = len(POOL_WINDOWS)
POOL_GROUP_DIM = POOL_WIDTH // POOL_GROUPS
D_FF = 4 * D_MODEL
IN_PROJ_WIDTH = 2 * RET_QK_WIDTH + 2 * RET_WIDTH + POOL_WIDTH
N_MOD = 6
ROPE_BASE = 10000.0
LN_EPS = 1e-5
DN_ALPHA = (2 * DEPTH) ** 0.25
DN_BETA = (8 * DEPTH) ** -0.25

kernel_name = "hymba_retnet_poolformer_deepnorm_adaln"


def layer_norm(x, gain=None, bias=None):
    xf = x.astype(jnp.float32)
    mu = jnp.mean(xf, axis=-1, keepdims=True)
    xc = xf - mu
    var = jnp.mean(xc * xc, axis=-1, keepdims=True)
    y = xc * lax.rsqrt(var + LN_EPS)
    if gain is not None:
        y = y * gain.astype(jnp.float32) + bias.astype(jnp.float32)
    return y.astype(x.dtype)


def rotary(t, pos):
    d = t.shape[-1]
    inv_freq = ROPE_BASE ** (-jnp.arange(0, d, 2, dtype=jnp.float32) / d)
    ang = pos.astype(jnp.float32)[..., None] * inv_freq
    cos = jnp.cos(ang)[:, :, None, :]
    sin = jnp.sin(ang)[:, :, None, :]
    t1, t2 = jnp.split(t, 2, axis=-1)
    return jnp.concatenate([t1 * cos - t2 * sin, t1 * sin + t2 * cos], axis=-1)


def to_chunks(t):
    b, s, h, d = t.shape
    return t.reshape(b, s // CHUNK, CHUNK, h, d).transpose(0, 3, 1, 2, 4)


def retention(q, k, v, pos):
    b, s, h, dk = q.shape
    dv = v.shape[-1]
    q = rotary(q.astype(jnp.float32), pos)
    k = rotary(k.astype(jnp.float32), pos) * (dk ** -0.5)
    v = v.astype(jnp.float32)
    log_gamma = jnp.log(jnp.asarray(1.0 - 2.0 ** (-5.0 - np.arange(h)), dtype=jnp.float32))
    idx = jnp.arange(CHUNK, dtype=jnp.float32)
    diff = idx[:, None] - idx[None, :]
    decay_mask = jnp.where(diff[None] >= 0,
                           jnp.exp(jnp.maximum(diff, 0.0)[None] * log_gamma[:, None, None]),
                           0.0)
    zeta = jnp.exp((CHUNK - 1.0 - idx)[None, :] * log_gamma[:, None])
    xi = jnp.exp((idx + 1.0)[None, :] * log_gamma[:, None])
    chunk_decay = jnp.exp(CHUNK * log_gamma)

    qc, kc, vc = to_chunks(q), to_chunks(k), to_chunks(v)
    scores = jnp.einsum('bhnid,bhnjd->bhnij', qc, kc) * decay_mask[None, :, None]
    intra = jnp.einsum('bhnij,bhnje->bhnie', scores, vc)

    kv = jnp.einsum('bhnjd,bhnje->bhnde', kc * zeta[None, :, None, :, None], vc)

    def step(state, kv_n):
        return state * chunk_decay[None, :, None, None] + kv_n, state

    _, states = lax.scan(step, jnp.zeros((b, h, dk, dv), jnp.float32), jnp.moveaxis(kv, 2, 0))
    states = jnp.moveaxis(states, 0, 2)
    cross = jnp.einsum('bhnid,bhnde->bhnie', qc * xi[None, :, None, :, None], states)
    o = intra + cross
    return o.transpose(0, 2, 3, 1, 4).reshape(b, s, h, dv)


def multiscale_pool(p, w_pool_l, pool_scale_l):
    b, s, _ = p.shape
    pg = p.astype(jnp.float32).reshape(b, s, POOL_GROUPS, POOL_GROUP_DIM)
    cs = jnp.cumsum(pg, axis=1)
    t1 = jnp.arange(1, s + 1, dtype=jnp.float32)
    outs = []
    for g, w in enumerate(POOL_WINDOWS):
        cs_g = cs[:, :, g]
        lagged = jnp.pad(cs_g, ((0, 0), (w, 0), (0, 0)))[:, :s]
        count = jnp.minimum(t1, float(w))[None, :, None]
        outs.append((cs_g - lagged) / count - pg[:, :, g])
    pooled = jnp.stack(outs, axis=2)
    mixed = jnp.einsum('bsgc,gcd->bsgd', pooled, w_pool_l.astype(jnp.float32))
    return (mixed.reshape(b, s, POOL_WIDTH) * pool_scale_l.astype(jnp.float32)).astype(p.dtype)


def token_mixer(u, pos, w_in_l, w_pool_l, pool_scale_l, w_out_l):
    b, s, _ = u.shape
    h = u @ w_in_l
    q, k, v, g, p = jnp.split(h, np.cumsum([RET_QK_WIDTH, RET_QK_WIDTH, RET_WIDTH, RET_WIDTH]).tolist(), axis=-1)
    q = q.reshape(b, s, RET_HEADS, RET_QK_DIM)
    k = k.reshape(b, s, RET_HEADS, RET_QK_DIM)
    v = v.reshape(b, s, RET_HEADS, RET_V_DIM)
    ret = layer_norm(retention(q, k, v, pos))
    ret = (jax.nn.silu(g.astype(jnp.float32)) * ret.reshape(b, s, RET_WIDTH)).astype(u.dtype)
    pool = multiscale_pool(p, w_pool_l, pool_scale_l)
    return jnp.concatenate([ret, pool], axis=-1) @ w_out_l


def setup_inputs(seed: int = 0) -> dict:
    key = jax.random.key(seed)
    ks = jax.random.split(key, 16)
    f32 = jnp.float32
    x = jax.random.normal(ks[0], (BATCH, SEQ, D_MODEL), f32)
    c = jax.random.normal(ks[1], (BATCH, D_MODEL), f32)
    offset = jax.random.randint(ks[2], (BATCH, 1), 0, 4096, dtype=jnp.int32)
    positions = (offset + jnp.arange(SEQ, dtype=jnp.int32)[None, :]).astype(jnp.int32)
    w_mod = jax.random.normal(ks[3], (D_MODEL, N_MOD * D_MODEL), f32) * (0.1 * D_MODEL ** -0.5)
    gate_base = jnp.asarray([0.0, 0.0, 1.0, 0.0, 0.0, 1.0], f32)[None, :, None]
    mod_table = gate_base + 0.02 * jax.random.normal(ks[4], (DEPTH, N_MOD, D_MODEL), f32)
    w_in = jax.random.normal(ks[5], (DEPTH, D_MODEL, IN_PROJ_WIDTH), f32) * (D_MODEL ** -0.5)
    w_pool = jax.random.normal(ks[6], (DEPTH, POOL_GROUPS, POOL_GROUP_DIM, POOL_GROUP_DIM), f32) * (POOL_GROUP_DIM ** -0.5)
    pool_scale = 1.0 + 0.1 * jax.random.normal(ks[7], (DEPTH, POOL_WIDTH), f32)
    w_out = jax.random.normal(ks[8], (DEPTH, MIX_WIDTH, D_MODEL), f32) * (DN_BETA * MIX_WIDTH ** -0.5)
    ln_mix_g = 1.0 + 0.02 * jax.random.normal(ks[9], (DEPTH, D_MODEL), f32)
    ln_mix_b = 0.02 * jax.random.normal(ks[10], (DEPTH, D_MODEL), f32)
    w_ff1 = jax.random.normal(ks[11], (DEPTH, D_MODEL, D_FF), f32) * (D_MODEL ** -0.5)
    w_ff2 = jax.random.normal(ks[12], (DEPTH, D_FF, D_MODEL), f32) * (DN_BETA * D_FF ** -0.5)
    ln_ff_g = 1.0 + 0.02 * jax.random.normal(ks[13], (DEPTH, D_MODEL), f32)
    ln_ff_b = 0.02 * jax.random.normal(ks[14], (DEPTH, D_MODEL), f32)
    return {"x": x, "c": c, "positions": positions, "w_mod": w_mod, "mod_table": mod_table,
            "w_in": w_in, "w_pool": w_pool, "pool_scale": pool_scale, "w_out": w_out,
            "ln_mix_g": ln_mix_g, "ln_mix_b": ln_mix_b, "w_ff1": w_ff1, "w_ff2": w_ff2,
            "ln_ff_g": ln_ff_g, "ln_ff_b": ln_ff_b}


def reference(x, c, positions, w_mod, mod_table, w_in, w_pool, pool_scale, w_out,
              ln_mix_g, ln_mix_b, w_ff1, w_ff2, ln_ff_g, ln_ff_b):
    b = x.shape[0]
    base_mod = (jax.nn.silu(c) @ w_mod).reshape(b, N_MOD, D_MODEL)
    for l in range(DEPTH):
        m = base_mod + mod_table[l][None]
        sh_a, sc_a, g_a, sh_f, sc_f, g_f = [m[:, i, None, :] for i in range(N_MOD)]
        u = layer_norm(x) * (1.0 + sc_a) + sh_a
        y = token_mixer(u, positions, w_in[l], w_pool[l], pool_scale[l], w_out[l])
        x = layer_norm(DN_ALPHA * x + g_a * y, ln_mix_g[l], ln_mix_b[l])
        u = layer_norm(x) * (1.0 + sc_f) + sh_f
        hdn = jnp.square(jax.nn.relu(u @ w_ff1[l]))
        x = layer_norm(DN_ALPHA * x + g_f * (hdn @ w_ff2[l]), ln_ff_g[l], ln_ff_b[l])
    return x
```

```python
import functools

import jax
import jax.numpy as jnp
from jax import lax
from jax.experimental import pallas as pl
from jax.experimental.pallas import tpu as pltpu

N_MOD = 6
QK_DIM = 128
V_DIM = 256
CHUNK = 128
POOL_WINDOWS = (2, 4, 8, 16)
POOL_HALO = 16
ROPE_BASE = 10000.0
LN_EPS = 1e-5
SUBLANES = 8
V7X_VMEM_CAP = 60000 * 1024

F32 = jnp.float32
BF16 = jnp.bfloat16


def _tile(dim, pref):
    t = min(dim, pref)
    while dim % t:
        t //= 2
    return t


def _params(semantics, vmem_bytes):
    return pltpu.CompilerParams(dimension_semantics=semantics,
                                vmem_limit_bytes=int(min(vmem_bytes, V7X_VMEM_CAP)))


def _ln(x):
    mu = jnp.mean(x, axis=-1, keepdims=True)
    xc = x - mu
    var = jnp.mean(xc * xc, axis=-1, keepdims=True)
    return xc * lax.rsqrt(var + LN_EPS)


def _modvec_kernel(c_ref, w_ref, tab_ref, o_ref):
    s = jax.nn.silu(c_ref[...])
    base = jnp.dot(s.astype(BF16), w_ref[...].astype(BF16), preferred_element_type=F32)
    o_ref[...] = base[None, :, :] + tab_ref[...]


def _modvec(c, w_mod, mod_table):
    b, d = c.shape
    depth = mod_table.shape[0]
    n = w_mod.shape[1]
    tn = _tile(n, 1024)
    c8 = jnp.zeros((SUBLANES, d), F32).at[:b].set(c)
    tab = mod_table.reshape(depth, 1, n)
    return pl.pallas_call(
        _modvec_kernel,
        grid=(n // tn,),
        in_specs=[pl.BlockSpec((SUBLANES, d), lambda j: (0, 0)),
                  pl.BlockSpec((d, tn), lambda j: (0, j)),
                  pl.BlockSpec((depth, 1, tn), lambda j: (0, 0, j))],
        out_specs=pl.BlockSpec((depth, SUBLANES, tn), lambda j: (0, 0, j)),
        out_shape=jax.ShapeDtypeStruct((depth, SUBLANES, n), F32),
        compiler_params=_params(("arbitrary",), 2 * d * tn * 4 + d * tn * 2 + (8 << 20)),
        name="modvec",
    )(c8, w_mod, tab)


def _rope_kernel(pos_ref, freq_ref, sign_ref, cos_ref, sin_ref):
    ang = pos_ref[...].astype(F32) * freq_ref[...]
    cos_ref[...] = jnp.cos(ang)
    sin_ref[...] = jnp.sin(ang) * sign_ref[...]


def _rope_tables(positions):
    m = positions.size
    half = QK_DIM // 2
    inv_freq = ROPE_BASE ** (-jnp.arange(0, QK_DIM, 2, dtype=F32) / QK_DIM)
    freq = jnp.concatenate([inv_freq, inv_freq]).reshape(1, QK_DIM)
    sign = jnp.concatenate([-jnp.ones((half,), F32), jnp.ones((half,), F32)]).reshape(1, QK_DIM)
    tr = _tile(m, 1024)
    vec = pl.BlockSpec((1, QK_DIM), lambda i: (0, 0))
    tab = pl.BlockSpec((tr, QK_DIM), lambda i: (i, 0))
    return pl.pallas_call(
        _rope_kernel,
        grid=(m // tr,),
        in_specs=[pl.BlockSpec((tr, 1), lambda i: (i, 0)), vec, vec],
        out_specs=[tab, tab],
        out_shape=[jax.ShapeDtypeStruct((m, QK_DIM), F32)] * 2,
        compiler_params=_params(("arbitrary",), 32 << 20),
        name="rope_tables",
    )(positions.reshape(m, 1), freq, sign)


def _ln_mod_kernel(x_ref, sc_ref, sh_ref, u_ref):
    u_ref[...] = (_ln(x_ref[...]) * (1.0 + sc_ref[...]) + sh_ref[...]).astype(u_ref.dtype)


def _ln_mod(x, sc, sh, seq):
    m, d = x.shape
    tm = _tile(seq, 512)
    per_seq = seq // tm
    vec = pl.BlockSpec((None, 1, d), lambda i: (i // per_seq, 0, 0))
    return pl.pallas_call(
        _ln_mod_kernel,
        grid=(m // tm,),
        in_specs=[pl.BlockSpec((tm, d), lambda i: (i, 0)), vec, vec],
        out_specs=pl.BlockSpec((tm, d), lambda i: (i, 0)),
        out_shape=jax.ShapeDtypeStruct((m, d), BF16),
        compiler_params=_params(("arbitrary",), 4 * tm * d * 4 + (8 << 20)),
        name="ln_mod",
    )(x, sc, sh)


def _mm_kernel(a_ref, b_ref, o_ref, *, relu2):
    acc = jnp.dot(a_ref[...], b_ref[...], preferred_element_type=F32)
    if relu2:
        acc = jnp.square(jnp.maximum(acc, 0.0))
    o_ref[...] = acc.astype(o_ref.dtype)


def _matmul(a, w, layer, n, *, relu2=False, out_dtype=BF16):
    m, k = a.shape
    tm = _tile(m, 1024)
    tn = _tile(n, 1024)
    out_b = jnp.dtype(out_dtype).itemsize
    vmem = 2 * (tm * k * 2 + k * tn * 2 + tm * tn * out_b) + 2 * tm * tn * 4 + (4 << 20)
    return pl.pallas_call(
        functools.partial(_mm_kernel, relu2=relu2),
        grid=(m // tm, n // tn),
        in_specs=[pl.BlockSpec((tm, k), lambda i, j: (i, 0)),
                  pl.BlockSpec((None, k, tn), lambda i, j: (layer, 0, j))],
        out_specs=pl.BlockSpec((tm, tn), lambda i, j: (i, j)),
        out_shape=jax.ShapeDtypeStruct((m, n), out_dtype),
        compiler_params=_params(("arbitrary", "arbitrary"), vmem),
        name="matmul_relu2" if relu2 else "matmul",
    )(a, w)


def _mm_acc_kernel(a_ref, b_ref, o_ref, acc_ref):
    kk = pl.program_id(2)

    @pl.when(kk == 0)
    def _():
        acc_ref[...] = jnp.zeros_like(acc_ref)

    acc_ref[...] += jnp.dot(a_ref[...], b_ref[...], preferred_element_type=F32)

    @pl.when(kk == pl.num_programs(2) - 1)
    def _():
        o_ref[...] = acc_ref[...].astype(o_ref.dtype)


def _matmul_acc(a, w, layer, *, out_dtype=F32):
    m, k = a.shape
    n = w.shape[2]
    tm = _tile(m, 1024)
    tn = _tile(n, 2048)
    tk = _tile(k, 1024)
    out_b = jnp.dtype(out_dtype).itemsize
    vmem = 2 * (tm * tk * 2 + tk * tn * 2 + tm * tn * out_b) + 2 * tm * tn * 4 + (4 << 20)
    return pl.pallas_call(
        _mm_acc_kernel,
        grid=(m // tm, n // tn, k // tk),
        in_specs=[pl.BlockSpec((tm, tk), lambda i, j, kk: (i, kk)),
                  pl.BlockSpec((None, tk, tn), lambda i, j, kk: (layer, kk, j))],
        out_specs=pl.BlockSpec((tm, tn), lambda i, j, kk: (i, j)),
        out_shape=jax.ShapeDtypeStruct((m, n), out_dtype),
        scratch_shapes=[pltpu.VMEM((tm, tn), F32)],
        compiler_params=_params(("arbitrary", "arbitrary", "arbitrary"), vmem),
        name="matmul_acc",
    )(a, w)


def _mm2_kernel(a1_ref, a2_ref, b1_ref, b2_ref, o_ref):
    acc = jnp.dot(a1_ref[...], b1_ref[...], preferred_element_type=F32)
    acc = acc + jnp.dot(a2_ref[...], b2_ref[...], preferred_element_type=F32)
    o_ref[...] = acc.astype(o_ref.dtype)


def _matmul_cat(a1, a2, w, layer, *, out_dtype=F32):
    m, k1 = a1.shape
    k2 = a2.shape[1]
    assert k1 == k2
    n = w.shape[2]
    tm = _tile(m, 1024)
    tn = _tile(n, 1024)
    out_b = jnp.dtype(out_dtype).itemsize
    vmem = 2 * (2 * tm * k1 * 2 + 2 * k1 * tn * 2 + tm * tn * out_b) + 2 * tm * tn * 4 + (4 << 20)
    return pl.pallas_call(
        _mm2_kernel,
        grid=(m // tm, n // tn),
        in_specs=[pl.BlockSpec((tm, k1), lambda i, j: (i, 0)),
                  pl.BlockSpec((tm, k2), lambda i, j: (i, 0)),
                  pl.BlockSpec((None, k1, tn), lambda i, j: (layer, 0, j)),
                  pl.BlockSpec((None, k2, tn), lambda i, j: (layer, 1, j))],
        out_specs=pl.BlockSpec((tm, tn), lambda i, j: (i, j)),
        out_shape=jax.ShapeDtypeStruct((m, n), out_dtype),
        compiler_params=_params(("arbitrary", "arbitrary"), vmem),
        name="matmul_cat",
    )(a1, a2, w, w)


def _pool_kernel(u_ref, wp_ref, wmix_ref, scale_ref, o_ref, pbuf, *, per_seq, tm, gdim):
    i = pl.program_id(0)
    first = (i % per_seq) == 0

    @pl.when(first)
    def _():
        pbuf[0:POOL_HALO, :] = jnp.zeros((POOL_HALO, pbuf.shape[1]), F32)

    @pl.when(jnp.logical_not(first))
    def _():
        pbuf[0:POOL_HALO, :] = pbuf[tm:tm + POOL_HALO, :]

    pbuf[POOL_HALO:POOL_HALO + tm, :] = jnp.dot(u_ref[...], wp_ref[...], preferred_element_type=F32)

    t1 = ((i % per_seq) * tm + 1 + lax.broadcasted_iota(jnp.int32, (tm, 1), 0)).astype(F32)
    for g, w in enumerate(POOL_WINDOWS):
        cols = slice(g * gdim, (g + 1) * gdim)
        cur = pbuf[POOL_HALO:POOL_HALO + tm, cols]
        win = cur
        for s in range(1, w):
            win = win + pbuf[POOL_HALO - s:POOL_HALO - s + tm, cols]
        pooled = win / jnp.minimum(t1, float(w)) - cur
        mixed = jnp.dot(pooled.astype(BF16), wmix_ref[g], preferred_element_type=F32)
        o_ref[:, cols] = (mixed * scale_ref[:, cols]).astype(o_ref.dtype)


def _pool_branch(u, w_in, w_pool, pool_scale, layer, seq):
    m, d = u.shape
    groups = len(POOL_WINDOWS)
    gdim = w_pool.shape[-1]
    pw = groups * gdim
    p_block = w_in.shape[2] // pw - 1
    tm = _tile(seq, 512)
    per_seq = seq // tm
    once = pl.Buffered(1)
    vmem = (2 * tm * d * 2 + d * pw * 2 + groups * gdim * gdim * 2 + 2 * tm * pw * 2
            + (tm + POOL_HALO) * pw * 4 + 3 * tm * pw * 4 + (4 << 20))
    return pl.pallas_call(
        functools.partial(_pool_kernel, per_seq=per_seq, tm=tm, gdim=gdim),
        grid=(m // tm,),
        in_specs=[pl.BlockSpec((tm, d), lambda i: (i, 0)),
                  pl.BlockSpec((None, d, pw), lambda i: (layer, 0, p_block), pipeline_mode=once),
                  pl.BlockSpec((None, groups, gdim, gdim), lambda i: (layer, 0, 0, 0), pipeline_mode=once),
                  pl.BlockSpec((None, 1, pw), lambda i: (layer, 0, 0), pipeline_mode=once)],
        out_specs=pl.BlockSpec((tm, pw), lambda i: (i, 0)),
        out_shape=jax.ShapeDtypeStruct((m, pw), BF16),
        scratch_shapes=[pltpu.VMEM((tm + POOL_HALO, pw), F32)],
        compiler_params=_params(("arbitrary",), vmem),
        name="pool_branch",
    )(u, w_in, w_pool, pool_scale.reshape(pool_scale.shape[0], 1, pw))


def _ret_kernel(q_ref, k_ref, v_ref, g_ref, cos_ref, sin_ref, o_ref, state_ref, *, heads, n_chunks):
    @pl.when(pl.program_id(1) == 0)
    def _():
        state_ref[...] = jnp.zeros_like(state_ref)

    c = CHUNK
    row = lax.broadcasted_iota(jnp.int32, (c, c), 0)
    col = lax.broadcasted_iota(jnp.int32, (c, c), 1)
    diff = (row - col).astype(F32)
    ridx = row.astype(F32)
    scale = QK_DIM ** -0.5
    for h in range(heads):
        gamma = 1.0 - 2.0 ** (-5.0 - h)
        lg = jnp.log(jnp.full((c, c), gamma, F32))
        decay = jnp.where(diff >= 0, jnp.exp(jnp.maximum(diff, 0.0) * lg), 0.0)
        zeta = jnp.exp((c - 1.0 - ridx) * lg)
        xi = jnp.exp((ridx + 1.0) * lg)
        cdecay = jnp.exp(c * jnp.log(jnp.full((QK_DIM, V_DIM), gamma, F32)))
        qs = slice(h * QK_DIM, (h + 1) * QK_DIM)
        vs = slice(h * V_DIM, (h + 1) * V_DIM)
        for n in range(n_chunks):
            rows = slice(n * c, (n + 1) * c)
            cos = cos_ref[rows, :]
            sin = sin_ref[rows, :]
            q = q_ref[rows, qs].astype(F32)
            k = k_ref[rows, qs].astype(F32)
            qr = q * cos + pltpu.roll(q, QK_DIM // 2, 1) * sin
            kr = (k * cos + pltpu.roll(k, QK_DIM // 2, 1) * sin) * scale
            v = v_ref[rows, vs]
            scores = lax.dot_general(qr.astype(BF16), kr.astype(BF16), (((1,), (1,)), ((), ())),
                                     preferred_element_type=F32) * decay
            kv = jnp.dot((kr * zeta).T.astype(BF16), v, preferred_element_type=F32)
            state = state_ref[h]
            out = (jnp.dot(scores.astype(BF16), v, preferred_element_type=F32)
                   + jnp.dot((qr * xi).astype(BF16), state.astype(BF16), preferred_element_type=F32))
            state_ref[h] = state * cdecay + kv
            gate = jax.nn.silu(g_ref[rows, vs].astype(F32))
            o_ref[rows, vs] = (gate * _ln(out)).astype(o_ref.dtype)


def _retention(h_qkvg, cos, sin, batch, seq):
    m, width = h_qkvg.shape
    rw = width // 3
    heads = rw // V_DIM
    qkw = heads * QK_DIM
    assert 2 * qkw == rw
    tr = _tile(seq, 4 * CHUNK)
    per_seq = seq // tr
    rowblk = lambda b, s: b * per_seq + s
    vmem = 2 * tr * (2 * qkw * 2 + 3 * rw * 2 + 2 * QK_DIM * 4) + heads * QK_DIM * V_DIM * 4 + (16 << 20)
    return pl.pallas_call(
        functools.partial(_ret_kernel, heads=heads, n_chunks=tr // CHUNK),
        grid=(batch, per_seq),
        in_specs=[pl.BlockSpec((tr, qkw), lambda b, s: (rowblk(b, s), 0)),
                  pl.BlockSpec((tr, qkw), lambda b, s: (rowblk(b, s), 1)),
                  pl.BlockSpec((tr, rw), lambda b, s: (rowblk(b, s), 1)),
                  pl.BlockSpec((tr, rw), lambda b, s: (rowblk(b, s), 2)),
                  pl.BlockSpec((tr, QK_DIM), lambda b, s: (rowblk(b, s), 0)),
                  pl.BlockSpec((tr, QK_DIM), lambda b, s: (rowblk(b, s), 0))],
        out_specs=pl.BlockSpec((tr, rw), lambda b, s: (rowblk(b, s), 0)),
        out_shape=jax.ShapeDtypeStruct((m, rw), BF16),
        scratch_shapes=[pltpu.VMEM((heads, QK_DIM, V_DIM), F32)],
        compiler_params=_params(("arbitrary", "arbitrary"), vmem),
        name="retention",
    )(h_qkvg, h_qkvg, h_qkvg, h_qkvg, cos, sin)


def _resid_kernel(x_ref, y_ref, gate_ref, gain_ref, bias_ref, sc_ref, sh_ref, xo_ref, uo_ref, *, alpha):
    z = alpha * x_ref[...] + gate_ref[...] * y_ref[...].astype(F32)
    xn = _ln(z) * gain_ref[...] + bias_ref[...]
    xo_ref[...] = xn
    uo_ref[...] = (_ln(xn) * (1.0 + sc_ref[...]) + sh_ref[...]).astype(uo_ref.dtype)


def _resid_norm(x, y, gate, gain, bias, sc, sh, seq, alpha):
    m, d = x.shape
    tm = _tile(seq, 256)
    per_seq = seq // tm
    row = pl.BlockSpec((tm, d), lambda i: (i, 0))
    per_batch = pl.BlockSpec((None, 1, d), lambda i: (i // per_seq, 0, 0))
    shared = pl.BlockSpec((1, d), lambda i: (0, 0))
    vmem = 2 * tm * d * (4 + y.dtype.itemsize + 4 + 2) + 6 * tm * d * 4 + (4 << 20)
    return pl.pallas_call(
        functools.partial(_resid_kernel, alpha=alpha),
        grid=(m // tm,),
        in_specs=[row, row, per_batch, shared, shared, per_batch, per_batch],
        out_specs=[row, row],
        out_shape=[jax.ShapeDtypeStruct((m, d), F32), jax.ShapeDtypeStruct((m, d), BF16)],
        compiler_params=_params(("arbitrary",), vmem),
        name="resid_norm",
    )(x, y, gate, gain.reshape(1, d), bias.reshape(1, d), sc, sh)


def kernel(x, c, positions, w_mod, mod_table, w_in, w_pool, pool_scale, w_out, ln_mix_g, ln_mix_b,
           w_ff1, w_ff2, ln_ff_g, ln_ff_b):
    batch, seq, d = x.shape
    depth = w_in.shape[0]
    m = batch * seq
    alpha = (2 * depth) ** 0.25
    rw = d // 2
    assert seq % CHUNK == 0 and rw % V_DIM == 0 and w_in.shape[2] == 4 * rw

    mods = _modvec(c, w_mod, mod_table).reshape(depth, SUBLANES, N_MOD, d)

    def mod(layer, which):
        return mods[layer, :batch, which][:, None, :]

    cos, sin = _rope_tables(positions)
    w_in_b, w_pool_b, w_out_b = w_in.astype(BF16), w_pool.astype(BF16), w_out.astype(BF16)
    w_ff1_b, w_ff2_b = w_ff1.astype(BF16), w_ff2.astype(BF16)

    xf = x.reshape(m, d)
    u = _ln_mod(xf, mod(0, 1), mod(0, 0), seq)
    for l in range(depth):
        h_qkvg = _matmul(u, w_in_b, l, 3 * rw)
        ret = _retention(h_qkvg, cos, sin, batch, seq)
        pool = _pool_branch(u, w_in_b, w_pool_b, pool_scale, l, seq)
        y = _matmul_cat(ret, pool, w_out_b, l)
        xf, u = _resid_norm(xf, y, mod(l, 2), ln_mix_g[l], ln_mix_b[l], mod(l, 4), mod(l, 3), seq, alpha)
        hdn = _matmul(u, w_ff1_b, l, w_ff1.shape[2], relu2=True)
        y = _matmul_acc(hdn, w_ff2_b, l)
        nxt = min(l + 1, depth - 1)
        xf, u = _resid_norm(xf, y, mod(l, 5), ln_ff_g[l], ln_ff_b[l], mod(nxt, 1), mod(nxt, 0), seq, alpha)
    return xf.reshape(batch, seq, d)
```

```python
import functools

import jax
import jax.numpy as jnp
from jax import lax
from jax.experimental import pallas as pl
from jax.experimental.pallas import tpu as pltpu

N_MOD = 6
QK_DIM = 128
V_DIM = 256
CHUNK = 128
POOL_WINDOWS = (2, 4, 8, 16)
POOL_HALO = 16
ROPE_BASE = 10000.0
LN_EPS = 1e-5
SUBLANES = 8
V7X_VMEM_CAP = 60000 * 1024

F32 = jnp.float32
BF16 = jnp.bfloat16


def _tile(dim, pref):
    t = min(dim, pref)
    while dim % t:
        t //= 2
    return t


def _params(semantics, vmem_bytes):
    return pltpu.CompilerParams(dimension_semantics=semantics,
                                vmem_limit_bytes=int(min(vmem_bytes, V7X_VMEM_CAP)))


def _ln(x):
    mu = jnp.mean(x, axis=-1, keepdims=True)
    xc = x - mu
    var = jnp.mean(xc * xc, axis=-1, keepdims=True)
    return xc * lax.rsqrt(var + LN_EPS)


def _modvec_kernel(c_ref, w_ref, tab_ref, o_ref):
    s = jax.nn.silu(c_ref[...])
    base = jnp.dot(s.astype(BF16), w_ref[...].astype(BF16), preferred_element_type=F32)
    o_ref[...] = base[None, :, :] + tab_ref[...]


def _modvec(c, w_mod, mod_table):
    b, d = c.shape
    depth = mod_table.shape[0]
    n = w_mod.shape[1]
    tn = _tile(n, 1024)
    c8 = jnp.zeros((SUBLANES, d), F32).at[:b].set(c)
    tab = mod_table.reshape(depth, 1, n)
    return pl.pallas_call(
        _modvec_kernel,
        grid=(n // tn,),
        in_specs=[pl.BlockSpec((SUBLANES, d), lambda j: (0, 0)),
                  pl.BlockSpec((d, tn), lambda j: (0, j)),
                  pl.BlockSpec((depth, 1, tn), lambda j: (0, 0, j))],
        out_specs=pl.BlockSpec((depth, SUBLANES, tn), lambda j: (0, 0, j)),
        out_shape=jax.ShapeDtypeStruct((depth, SUBLANES, n), F32),
        compiler_params=_params(("arbitrary",), 2 * d * tn * 4 + d * tn * 2 + (8 << 20)),
        name="modvec",
    )(c8, w_mod, tab)


def _rope_kernel(pos_ref, freq_ref, sign_ref, cq_ref, sq_ref, ck_ref, sk_ref):
    ang = pos_ref[...].astype(F32) * freq_ref[...]
    cos = jnp.cos(ang)
    sin = jnp.sin(ang) * sign_ref[...]
    scale = QK_DIM ** -0.5
    cq_ref[...] = cos
    sq_ref[...] = sin
    ck_ref[...] = cos * scale
    sk_ref[...] = sin * scale


def _rope_tables(positions):
    m = positions.size
    half = QK_DIM // 2
    inv_freq = ROPE_BASE ** (-jnp.arange(0, QK_DIM, 2, dtype=F32) / QK_DIM)
    freq = jnp.concatenate([inv_freq, inv_freq]).reshape(1, QK_DIM)
    sign = jnp.concatenate([-jnp.ones((half,), F32), jnp.ones((half,), F32)]).reshape(1, QK_DIM)
    tr = _tile(m, 1024)
    vec = pl.BlockSpec((1, QK_DIM), lambda i: (0, 0))
    tab = pl.BlockSpec((tr, QK_DIM), lambda i: (i, 0))
    return pl.pallas_call(
        _rope_kernel,
        grid=(m // tr,),
        in_specs=[pl.BlockSpec((tr, 1), lambda i: (i, 0)), vec, vec],
        out_specs=[tab] * 4,
        out_shape=[jax.ShapeDtypeStruct((m, QK_DIM), F32)] * 4,
        compiler_params=_params(("arbitrary",), 32 << 20),
        name="rope_tables",
    )(positions.reshape(m, 1), freq, sign)


def _ln_mod_kernel(x_ref, sc_ref, sh_ref, u_ref):
    u_ref[...] = (_ln(x_ref[...]) * (1.0 + sc_ref[...]) + sh_ref[...]).astype(u_ref.dtype)


def _ln_mod(x, sc, sh, seq):
    m, d = x.shape
    tm = _tile(seq, 512)
    per_seq = seq // tm
    vec = pl.BlockSpec((None, 1, d), lambda i: (i // per_seq, 0, 0))
    return pl.pallas_call(
        _ln_mod_kernel,
        grid=(m // tm,),
        in_specs=[pl.BlockSpec((tm, d), lambda i: (i, 0)), vec, vec],
        out_specs=pl.BlockSpec((tm, d), lambda i: (i, 0)),
        out_shape=jax.ShapeDtypeStruct((m, d), BF16),
        compiler_params=_params(("arbitrary",), 4 * tm * d * 4 + (8 << 20)),
        name="ln_mod",
    )(x, sc, sh)


def _mm_kernel(a_ref, b_ref, o_ref, *, relu2):
    acc = jnp.dot(a_ref[...], b_ref[...], preferred_element_type=F32)
    if relu2:
        acc = jnp.square(jnp.maximum(acc, 0.0))
    o_ref[...] = acc.astype(o_ref.dtype)


def _matmul(a, w, layer, n, *, relu2=False, out_dtype=BF16):
    m, k = a.shape
    tm = _tile(m, 1024)
    tn = _tile(n, 1024)
    out_b = jnp.dtype(out_dtype).itemsize
    vmem = 2 * (tm * k * 2 + k * tn * 2 + tm * tn * out_b) + 2 * tm * tn * 4 + (4 << 20)
    return pl.pallas_call(
        functools.partial(_mm_kernel, relu2=relu2),
        grid=(m // tm, n // tn),
        in_specs=[pl.BlockSpec((tm, k), lambda i, j: (i, 0)),
                  pl.BlockSpec((None, k, tn), lambda i, j: (layer, 0, j))],
        out_specs=pl.BlockSpec((tm, tn), lambda i, j: (i, j)),
        out_shape=jax.ShapeDtypeStruct((m, n), out_dtype),
        compiler_params=_params(("arbitrary", "arbitrary"), vmem),
        name="matmul_relu2" if relu2 else "matmul",
    )(a, w)


def _mm_cast_kernel(a_ref, b_ref, *refs, relu2, n_cast):
    _mm_kernel(a_ref, b_ref, refs[n_cast], relu2=relu2)
    for src, dst in zip(refs[:n_cast], refs[n_cast + 1:]):
        dst[...] = src[...].astype(dst.dtype)


def _matmul_and_cast(a, w, layer, n, casts, cast_layer, *, relu2=False, out_dtype=BF16):
    m, k = a.shape
    tm = _tile(m, 1024)
    tn = _tile(n, 1024)
    nj = n // tn
    steps = (m // tm) * nj
    out_b = jnp.dtype(out_dtype).itemsize
    vmem = 2 * (tm * k * 2 + k * tn * 2 + tm * tn * out_b) + 2 * tm * tn * 4 + (4 << 20)
    in_specs = [pl.BlockSpec((tm, k), lambda i, j: (i, 0)),
                pl.BlockSpec((None, k, tn), lambda i, j: (layer, 0, j))]
    out_specs = [pl.BlockSpec((tm, tn), lambda i, j: (i, j))]
    out_shape = [jax.ShapeDtypeStruct((m, n), out_dtype)]
    for wf in casts:
        _, r, c = wf.shape
        rows = r // steps
        assert rows * steps == r and rows % 16 == 0, (wf.shape, steps)
        in_specs.append(pl.BlockSpec((None, rows, c), lambda i, j: (cast_layer, i * nj + j, 0)))
        out_specs.append(pl.BlockSpec((None, rows, c), lambda i, j: (0, i * nj + j, 0)))
        out_shape.append(jax.ShapeDtypeStruct((1, r, c), BF16))
        vmem += 2 * rows * c * (4 + 2)
    outs = pl.pallas_call(
        functools.partial(_mm_cast_kernel, relu2=relu2, n_cast=len(casts)),
        grid=(m // tm, nj),
        in_specs=in_specs,
        out_specs=out_specs,
        out_shape=out_shape,
        compiler_params=_params(("arbitrary", "arbitrary"), vmem),
        name="matmul_relu2_cast" if relu2 else "matmul_cast",
    )(a, w, *casts)
    return outs[0], outs[1:]


def _mm_acc_kernel(a_ref, b_ref, o_ref, acc_ref):
    kk = pl.program_id(2)

    @pl.when(kk == 0)
    def _():
        acc_ref[...] = jnp.zeros_like(acc_ref)

    acc_ref[...] += jnp.dot(a_ref[...], b_ref[...], preferred_element_type=F32)

    @pl.when(kk == pl.num_programs(2) - 1)
    def _():
        o_ref[...] = acc_ref[...].astype(o_ref.dtype)


def _matmul_acc(a, w, layer, *, out_dtype=BF16):
    m, k = a.shape
    n = w.shape[2]
    tm = _tile(m, 1024)
    tn = _tile(n, 2048)
    tk = _tile(k, 2048)
    out_b = jnp.dtype(out_dtype).itemsize
    vmem = 2 * (tm * tk * 2 + tk * tn * 2 + tm * tn * out_b) + 2 * tm * tn * 4 + (4 << 20)
    return pl.pallas_call(
        _mm_acc_kernel,
        grid=(m // tm, n // tn, k // tk),
        in_specs=[pl.BlockSpec((tm, tk), lambda i, j, kk: (i, kk)),
                  pl.BlockSpec((None, tk, tn), lambda i, j, kk: (layer, kk, j))],
        out_specs=pl.BlockSpec((tm, tn), lambda i, j, kk: (i, j)),
        out_shape=jax.ShapeDtypeStruct((m, n), out_dtype),
        scratch_shapes=[pltpu.VMEM((tm, tn), F32)],
        compiler_params=_params(("arbitrary", "arbitrary", "arbitrary"), vmem),
        name="matmul_acc",
    )(a, w)


def _mm2_kernel(a1_ref, a2_ref, b1_ref, b2_ref, o_ref):
    acc = jnp.dot(a1_ref[...], b1_ref[...], preferred_element_type=F32)
    acc = acc + jnp.dot(a2_ref[...], b2_ref[...], preferred_element_type=F32)
    o_ref[...] = acc.astype(o_ref.dtype)


def _matmul_cat(a1, a2, w, layer, *, out_dtype=BF16):
    m, k1 = a1.shape
    k2 = a2.shape[1]
    assert k1 == k2
    n = w.shape[2]
    tm = _tile(m, 1024)
    tn = _tile(n, 1024)
    out_b = jnp.dtype(out_dtype).itemsize
    vmem = 2 * (2 * tm * k1 * 2 + 2 * k1 * tn * 2 + tm * tn * out_b) + 2 * tm * tn * 4 + (4 << 20)
    return pl.pallas_call(
        _mm2_kernel,
        grid=(m // tm, n // tn),
        in_specs=[pl.BlockSpec((tm, k1), lambda i, j: (i, 0)),
                  pl.BlockSpec((tm, k2), lambda i, j: (i, 0)),
                  pl.BlockSpec((None, k1, tn), lambda i, j: (layer, 0, j)),
                  pl.BlockSpec((None, k2, tn), lambda i, j: (layer, 1, j))],
        out_specs=pl.BlockSpec((tm, tn), lambda i, j: (i, j)),
        out_shape=jax.ShapeDtypeStruct((m, n), out_dtype),
        compiler_params=_params(("arbitrary", "arbitrary"), vmem),
        name="matmul_cat",
    )(a1, a2, w, w)


def _pool_kernel(u_ref, wp_ref, wmix_ref, scale_ref, o_ref, carry, *, per_seq, tm, gdim):
    i = pl.program_id(0)

    @pl.when((i % per_seq) == 0)
    def _():
        carry[...] = jnp.zeros_like(carry)

    t1 = ((i % per_seq) * tm + 1 + lax.broadcasted_iota(jnp.int32, (tm, 1), 0)).astype(F32)
    u = u_ref[...]
    for g, w in enumerate(POOL_WINDOWS):
        cols = slice(g * gdim, (g + 1) * gdim)
        p = jnp.dot(u, wp_ref[:, cols], preferred_element_type=F32)
        win = jnp.concatenate([carry[:, cols], p], axis=0)
        carry[:, cols] = p[tm - POOL_HALO:, :]
        span = 1
        while span < w:
            win = win + pltpu.roll(win, span, 0)
            span *= 2
        pooled = win[POOL_HALO:, :] / jnp.minimum(t1, float(w)) - p
        mixed = jnp.dot(pooled.astype(BF16), wmix_ref[g], preferred_element_type=F32)
        o_ref[:, cols] = (mixed * scale_ref[:, cols]).astype(o_ref.dtype)


def _pool_branch(u, w_in, w_pool, pool_scale, layer, seq):
    m, d = u.shape
    groups = len(POOL_WINDOWS)
    gdim = w_pool.shape[-1]
    pw = groups * gdim
    p_block = w_in.shape[2] // pw - 1
    tm = _tile(seq, 512)
    per_seq = seq // tm
    once = pl.Buffered(1)
    vmem = (2 * tm * d * 2 + d * pw * 2 + groups * gdim * gdim * 2 + 2 * tm * pw * 2
            + POOL_HALO * pw * 4 + 8 * (tm + POOL_HALO) * gdim * 4 + (4 << 20))
    return pl.pallas_call(
        functools.partial(_pool_kernel, per_seq=per_seq, tm=tm, gdim=gdim),
        grid=(m // tm,),
        in_specs=[pl.BlockSpec((tm, d), lambda i: (i, 0)),
                  pl.BlockSpec((None, d, pw), lambda i: (0, 0, p_block), pipeline_mode=once),
                  pl.BlockSpec((None, groups, gdim, gdim), lambda i: (layer, 0, 0, 0), pipeline_mode=once),
                  pl.BlockSpec((None, 1, pw), lambda i: (layer, 0, 0), pipeline_mode=once)],
        out_specs=pl.BlockSpec((tm, pw), lambda i: (i, 0)),
        out_shape=jax.ShapeDtypeStruct((m, pw), BF16),
        scratch_shapes=[pltpu.VMEM((POOL_HALO, pw), F32)],
        compiler_params=_params(("arbitrary",), vmem),
        name="pool_branch",
    )(u, w_in, w_pool, pool_scale.reshape(pool_scale.shape[0], 1, pw))


def _ret_kernel(q_ref, k_ref, v_ref, g_ref, cq_ref, sq_ref, ck_ref, sk_ref, o_ref, state_ref, tab_ref,
                *, heads, n_chunks):
    c = CHUNK

    @pl.when((pl.program_id(0) == 0) & (pl.program_id(1) == 0))
    def _():
        row = lax.broadcasted_iota(jnp.int32, (c, c), 0)
        col = lax.broadcasted_iota(jnp.int32, (c, c), 1)
        diff = (row - col).astype(F32)
        ridx = row.astype(F32)
        for h in range(heads):
            lg = jnp.log(jnp.full((c, c), 1.0 - 2.0 ** (-5.0 - h), F32))
            tab_ref[h, 0] = jnp.where(diff >= 0, jnp.exp(jnp.maximum(diff, 0.0) * lg), 0.0)
            tab_ref[h, 1] = jnp.exp((c - 1.0 - ridx) * lg)
            tab_ref[h, 2] = jnp.exp((ridx + 1.0) * lg)

    @pl.when(pl.program_id(1) == 0)
    def _():
        state_ref[...] = jnp.zeros_like(state_ref)

    for h in range(heads):
        cdecay = jnp.exp(c * jnp.log(jnp.full((1, V_DIM), 1.0 - 2.0 ** (-5.0 - h), F32)))
        qs = slice(h * QK_DIM, (h + 1) * QK_DIM)
        vs = slice(h * V_DIM, (h + 1) * V_DIM)
        for n in range(n_chunks):
            rows = slice(n * c, (n + 1) * c)
            q = q_ref[rows, qs].astype(F32)
            k = k_ref[rows, qs].astype(F32)
            qr = q * cq_ref[rows, :] + pltpu.roll(q, QK_DIM // 2, 1) * sq_ref[rows, :]
            kr = k * ck_ref[rows, :] + pltpu.roll(k, QK_DIM // 2, 1) * sk_ref[rows, :]
            v = v_ref[rows, vs]
            scores = lax.dot_general(qr.astype(BF16), kr.astype(BF16), (((1,), (1,)), ((), ())),
                                     preferred_element_type=F32) * tab_ref[h, 0]
            kv = jnp.dot((kr * tab_ref[h, 1]).astype(BF16).T, v, preferred_element_type=F32)
            state = state_ref[h]
            lhs = jnp.concatenate([scores.astype(BF16), (qr * tab_ref[h, 2]).astype(BF16)], axis=1)
            rhs = jnp.concatenate([v, state.astype(BF16)], axis=0)
            out = jnp.dot(lhs, rhs, preferred_element_type=F32)
            state_ref[h] = state * cdecay + kv
            gate = jax.nn.silu(g_ref[rows, vs].astype(F32))
            o_ref[rows, vs] = (gate * _ln(out)).astype(o_ref.dtype)


def _retention(h_qkvg, tables, batch, seq):
    m, width = h_qkvg.shape
    rw = width // 3
    heads = rw // V_DIM
    qkw = heads * QK_DIM
    assert 2 * qkw == rw and CHUNK == QK_DIM
    tr = _tile(seq, 4 * CHUNK)
    per_seq = seq // tr
    rowblk = lambda b, s: b * per_seq + s
    tab = pl.BlockSpec((tr, QK_DIM), lambda b, s: (rowblk(b, s), 0))
    vmem = (2 * tr * (2 * qkw * 2 + 3 * rw * 2 + 4 * QK_DIM * 4) + heads * QK_DIM * V_DIM * 4
            + heads * 3 * CHUNK * CHUNK * 4 + (16 << 20))
    return pl.pallas_call(
        functools.partial(_ret_kernel, heads=heads, n_chunks=tr // CHUNK),
        grid=(batch, per_seq),
        in_specs=[pl.BlockSpec((tr, qkw), lambda b, s: (rowblk(b, s), 0)),
                  pl.BlockSpec((tr, qkw), lambda b, s: (rowblk(b, s), 1)),
                  pl.BlockSpec((tr, rw), lambda b, s: (rowblk(b, s), 1)),
                  pl.BlockSpec((tr, rw), lambda b, s: (rowblk(b, s), 2)),
                  tab, tab, tab, tab],
        out_specs=pl.BlockSpec((tr, rw), lambda b, s: (rowblk(b, s), 0)),
        out_shape=jax.ShapeDtypeStruct((m, rw), BF16),
        scratch_shapes=[pltpu.VMEM((heads, QK_DIM, V_DIM), F32),
                        pltpu.VMEM((heads, 3, CHUNK, CHUNK), F32)],
        compiler_params=_params(("arbitrary", "arbitrary"), vmem),
        name="retention",
    )(h_qkvg, h_qkvg, h_qkvg, h_qkvg, *tables)


def _resid_kernel(x_ref, y_ref, gate_ref, gain_ref, bias_ref, *refs, alpha):
    z = alpha * x_ref[...] + gate_ref[...] * y_ref[...].astype(F32)
    xn = _ln(z) * gain_ref[...] + bias_ref[...]
    if len(refs) == 1:
        refs[0][...] = xn
    else:
        sc_ref, sh_ref, xo_ref, uo_ref = refs
        xo_ref[...] = xn
        uo_ref[...] = (_ln(xn) * (1.0 + sc_ref[...]) + sh_ref[...]).astype(uo_ref.dtype)


def _resid_norm(x, y, gate, gain, bias, next_mod, seq, alpha):
    m, d = x.shape
    tm = _tile(seq, 256)
    per_seq = seq // tm
    row = pl.BlockSpec((tm, d), lambda i: (i, 0))
    per_batch = pl.BlockSpec((None, 1, d), lambda i: (i // per_seq, 0, 0))
    shared = pl.BlockSpec((1, d), lambda i: (0, 0))
    vmem = 2 * tm * d * (4 + y.dtype.itemsize + 4 + 2) + 6 * tm * d * 4 + (4 << 20)
    x_out = jax.ShapeDtypeStruct((m, d), F32)
    extra = list(next_mod) if next_mod else []
    return pl.pallas_call(
        functools.partial(_resid_kernel, alpha=alpha),
        grid=(m // tm,),
        in_specs=[row, row, per_batch, shared, shared] + [per_batch] * len(extra),
        out_specs=[row, row] if next_mod else row,
        out_shape=[x_out, jax.ShapeDtypeStruct((m, d), BF16)] if next_mod else x_out,
        compiler_params=_params(("arbitrary",), vmem),
        name="resid_norm",
    )(x, y, gate, gain.reshape(1, d), bias.reshape(1, d), *extra)


def kernel(x, c, positions, w_mod, mod_table, w_in, w_pool, pool_scale, w_out, ln_mix_g, ln_mix_b,
           w_ff1, w_ff2, ln_ff_g, ln_ff_b):
    batch, seq, d = x.shape
    depth = w_in.shape[0]
    m = batch * seq
    alpha = (2 * depth) ** 0.25
    rw = d // 2
    assert seq % CHUNK == 0 and rw % V_DIM == 0 and w_in.shape[2] == 4 * rw

    mods = _modvec(c, w_mod, mod_table).reshape(depth, SUBLANES, N_MOD, d)

    def mod(layer, which):
        return mods[layer, :batch, which][:, None, :]

    tables = _rope_tables(positions)
    big_weights = [w_in, w_out, w_ff1, w_ff2]
    w_in_b, w_out_b, w_ff1_b, w_ff2_b = [w[:1].astype(BF16) for w in big_weights]
    w_pool_b = w_pool.astype(BF16)

    xf = x.reshape(m, d)
    u = _ln_mod(xf, mod(0, 1), mod(0, 0), seq)
    for l in range(depth):
        h_qkvg = _matmul(u, w_in_b, 0, 3 * rw)
        ret = _retention(h_qkvg, tables, batch, seq)
        pool = _pool_branch(u, w_in_b, w_pool_b, pool_scale, l, seq)
        y = _matmul_cat(ret, pool, w_out_b, 0)
        xf, u = _resid_norm(xf, y, mod(l, 2), ln_mix_g[l], ln_mix_b[l], (mod(l, 4), mod(l, 3)), seq, alpha)
        if l + 1 < depth:
            hdn, (w_in_b, w_out_b, w_ff1_b, next_ff2_b) = _matmul_and_cast(
                u, w_ff1_b, 0, w_ff1.shape[2], big_weights, l + 1, relu2=True)
        else:
            hdn = _matmul(u, w_ff1_b, 0, w_ff1.shape[2], relu2=True)
        y = _matmul_acc(hdn, w_ff2_b, 0)
        if l + 1 < depth:
            w_ff2_b = next_ff2_b
            xf, u = _resid_norm(xf, y, mod(l, 5), ln_ff_g[l], ln_ff_b[l], (mod(l + 1, 1), mod(l + 1, 0)),
                                seq, alpha)
        else:
            xf = _resid_norm(xf, y, mod(l, 5), ln_ff_g[l], ln_ff_b[l], None, seq, alpha)
    return xf.reshape(batch, seq, d)
```

```python
import functools

import jax
import jax.numpy as jnp
from jax import lax
from jax.experimental import pallas as pl
from jax.experimental.pallas import tpu as pltpu

N_MOD = 6
QK_DIM = 128
V_DIM = 256
CHUNK = 128
POOL_WINDOWS = (2, 4, 8, 16)
POOL_HALO = 16
ROPE_BASE = 10000.0
LN_EPS = 1e-5
SUBLANES = 8
V7X_VMEM_CAP = 60000 * 1024
V7X_MXU_WIDTH = 256

F32 = jnp.float32
BF16 = jnp.bfloat16


def _tile(dim, pref):
    t = min(dim, pref)
    while dim % t:
        t //= 2
    return t


def _params(semantics, vmem_bytes):
    return pltpu.CompilerParams(dimension_semantics=semantics,
                                vmem_limit_bytes=int(min(vmem_bytes, V7X_VMEM_CAP)))


def _ln(x):
    mu = jnp.mean(x, axis=-1, keepdims=True)
    xc = x - mu
    var = jnp.mean(xc * xc, axis=-1, keepdims=True)
    return xc * lax.rsqrt(var + LN_EPS)


def _modvec_kernel(c_ref, w_ref, tab_ref, o_ref):
    s = jax.nn.silu(c_ref[...])
    base = jnp.dot(s.astype(BF16), w_ref[...].astype(BF16), preferred_element_type=F32)
    o_ref[...] = base[None, :, :] + tab_ref[...]


def _modvec(c, w_mod, mod_table):
    b, d = c.shape
    depth = mod_table.shape[0]
    n = w_mod.shape[1]
    tn = _tile(n, 1024)
    c8 = jnp.zeros((SUBLANES, d), F32).at[:b].set(c)
    tab = mod_table.reshape(depth, 1, n)
    return pl.pallas_call(
        _modvec_kernel,
        grid=(n // tn,),
        in_specs=[pl.BlockSpec((SUBLANES, d), lambda j: (0, 0)),
                  pl.BlockSpec((d, tn), lambda j: (0, j)),
                  pl.BlockSpec((depth, 1, tn), lambda j: (0, 0, j))],
        out_specs=pl.BlockSpec((depth, SUBLANES, tn), lambda j: (0, 0, j)),
        out_shape=jax.ShapeDtypeStruct((depth, SUBLANES, n), F32),
        compiler_params=_params(("arbitrary",), 2 * d * tn * 4 + d * tn * 2 + (8 << 20)),
        name="modvec",
    )(c8, w_mod, tab)


def _rope_kernel(pos_ref, freq_ref, sign_ref, cq_ref, sq_ref, ck_ref, sk_ref):
    ang = pos_ref[...].astype(F32) * freq_ref[...]
    cos = jnp.cos(ang)
    sin = jnp.sin(ang) * sign_ref[...]
    scale = QK_DIM ** -0.5
    cq_ref[...] = cos
    sq_ref[...] = sin
    ck_ref[...] = cos * scale
    sk_ref[...] = sin * scale


def _rope_tables(positions):
    m = positions.size
    half = QK_DIM // 2
    inv_freq = ROPE_BASE ** (-jnp.arange(0, QK_DIM, 2, dtype=F32) / QK_DIM)
    freq = jnp.concatenate([inv_freq, inv_freq]).reshape(1, QK_DIM)
    sign = jnp.concatenate([-jnp.ones((half,), F32), jnp.ones((half,), F32)]).reshape(1, QK_DIM)
    tr = _tile(m, 1024)
    vec = pl.BlockSpec((1, QK_DIM), lambda i: (0, 0))
    tab = pl.BlockSpec((tr, QK_DIM), lambda i: (i, 0))
    return pl.pallas_call(
        _rope_kernel,
        grid=(m // tr,),
        in_specs=[pl.BlockSpec((tr, 1), lambda i: (i, 0)), vec, vec],
        out_specs=[tab] * 4,
        out_shape=[jax.ShapeDtypeStruct((m, QK_DIM), F32)] * 4,
        compiler_params=_params(("arbitrary",), 32 << 20),
        name="rope_tables",
    )(positions.reshape(m, 1), freq, sign)


def _ln_mod_kernel(x_ref, sc_ref, sh_ref, u_ref):
    u_ref[...] = (_ln(x_ref[...]) * (1.0 + sc_ref[...]) + sh_ref[...]).astype(u_ref.dtype)


def _ln_mod(x, sc, sh, seq):
    m, d = x.shape
    tm = _tile(seq, 512)
    per_seq = seq // tm
    vec = pl.BlockSpec((None, 1, d), lambda i: (i // per_seq, 0, 0))
    return pl.pallas_call(
        _ln_mod_kernel,
        grid=(m // tm,),
        in_specs=[pl.BlockSpec((tm, d), lambda i: (i, 0)), vec, vec],
        out_specs=pl.BlockSpec((tm, d), lambda i: (i, 0)),
        out_shape=jax.ShapeDtypeStruct((m, d), BF16),
        compiler_params=_params(("arbitrary",), 4 * tm * d * 4 + (8 << 20)),
        name="ln_mod",
    )(x, sc, sh)


def _mm_kernel(a_ref, b_ref, o_ref):
    o_ref[...] = jnp.dot(a_ref[...], b_ref[...], preferred_element_type=F32).astype(o_ref.dtype)


def _matmul(a, w, layer, n, *, out_dtype=BF16):
    m, k = a.shape
    tm = _tile(m, 1024)
    tn = _tile(n, 1024)
    out_b = jnp.dtype(out_dtype).itemsize
    vmem = 2 * (tm * k * 2 + k * tn * 2 + tm * tn * out_b) + 2 * tm * tn * 4 + (4 << 20)
    return pl.pallas_call(
        _mm_kernel,
        grid=(m // tm, n // tn),
        in_specs=[pl.BlockSpec((tm, k), lambda i, j: (i, 0)),
                  pl.BlockSpec((None, k, tn), lambda i, j: (layer, 0, j))],
        out_specs=pl.BlockSpec((tm, tn), lambda i, j: (i, j)),
        out_shape=jax.ShapeDtypeStruct((m, n), out_dtype),
        compiler_params=_params(("arbitrary", "arbitrary"), vmem),
        name="matmul",
    )(a, w)


def _resid_math(x, y, gate, gain, bias, alpha):
    return _ln(alpha * x + gate * y.astype(F32)) * gain + bias


def _norm_mm_kernel(x_ref, y_ref, gate_ref, gain_ref, bias_ref, sc_ref, sh_ref, w_ref, xo_ref, o_ref, *refs,
                    alpha, rows, relu2):
    u0, u1 = refs[-2:]
    i = pl.program_id(0)
    j = pl.program_id(1)

    def norm_rows(u_next):
        xn = _resid_math(x_ref[...], y_ref[...], gate_ref[...], gain_ref[...], bias_ref[...], alpha)
        xo_ref[...] = xn
        u = (_ln(xn) * (1.0 + sc_ref[...]) + sh_ref[...]).astype(u_next.dtype)
        u_next[pl.ds(pl.multiple_of(j * rows, rows), rows), :] = u
        if len(refs) == 3:
            refs[0][...] = u

    @pl.when(i == 0)
    def _():
        norm_rows(u0)

    for parity, (u_cur, u_next) in enumerate(((u1, u0), (u0, u1))):
        @pl.when((i > 0) & (i % 2 == parity))
        def _():
            acc = jnp.dot(u_cur[...], w_ref[...], preferred_element_type=F32)
            if relu2:
                acc = jnp.square(jnp.maximum(acc, 0.0))
            o_ref[...] = acc.astype(o_ref.dtype)
            norm_rows(u_next)


def _norm_matmul(x, y, gate, gain, bias, sc, sh, w, n, seq, alpha, *, relu2, emit_u):
    m, d = x.shape
    tm = _tile(seq, 1024)
    nm = m // tm
    tn = next(t for t in range(1024, 0, -V7X_MXU_WIDTH)
              if n % t == 0 and tm % (n // t) == 0 and (tm // (n // t)) % 16 == 0)
    nj = n // tn
    rows = tm // nj
    per_seq = seq // tm
    slab = pl.BlockSpec((rows, d), lambda i, j: (jnp.where(i < nm, i * nj + j, nm * nj - 1), 0))
    per_batch = pl.BlockSpec((None, 1, d), lambda i, j: (jnp.minimum(i, nm - 1) // per_seq, 0, 0))
    shared = pl.BlockSpec((1, d), lambda i, j: (0, 0))
    col = lambda i, j: jnp.where(i > 0, j, 0)
    vmem = (2 * tm * d * 2 + 2 * d * tn * 2 + 2 * tm * tn * 2 + 3 * tm * tn * 4
            + 2 * rows * d * (4 + 2 + 4 + 2) + 8 * rows * d * 4 + (4 << 20))
    out_specs = [slab, pl.BlockSpec((tm, tn), lambda i, j: (jnp.maximum(i - 1, 0), col(i, j)))]
    out_shape = [jax.ShapeDtypeStruct((m, d), F32), jax.ShapeDtypeStruct((m, n), BF16)]
    if emit_u:
        out_specs.append(slab)
        out_shape.append(jax.ShapeDtypeStruct((m, d), BF16))
    return pl.pallas_call(
        functools.partial(_norm_mm_kernel, alpha=alpha, rows=rows, relu2=relu2),
        grid=(nm + 1, nj),
        in_specs=[slab, slab, per_batch, shared, shared, per_batch, per_batch,
                  pl.BlockSpec((None, d, tn), lambda i, j: (0, 0, col(i, j)))],
        out_specs=out_specs,
        out_shape=out_shape,
        scratch_shapes=[pltpu.VMEM((tm, d), BF16), pltpu.VMEM((tm, d), BF16)],
        compiler_params=_params(("arbitrary", "arbitrary"), vmem),
        name="norm_matmul_relu2" if relu2 else "norm_matmul",
    )(x, y, gate, gain.reshape(1, d), bias.reshape(1, d), sc, sh, w)


def _mm_acc_kernel(a_ref, b_ref, *refs, n_cast):
    o_ref, acc_ref = refs[n_cast], refs[-1]
    kk = pl.program_id(2)

    @pl.when(kk == 0)
    def _():
        acc_ref[...] = jnp.zeros_like(acc_ref)

    acc_ref[...] += jnp.dot(a_ref[...], b_ref[...], preferred_element_type=F32)
    for src, dst in zip(refs[:n_cast], refs[n_cast + 1:-1]):
        dst[...] = src[...].astype(dst.dtype)

    @pl.when(kk == pl.num_programs(2) - 1)
    def _():
        o_ref[...] = acc_ref[...].astype(o_ref.dtype)


def _matmul_acc(a, w, layer, casts=(), cast_layer=None, *, out_dtype=BF16):
    m, k = a.shape
    n = w.shape[2]
    tm = _tile(m, 1024)
    tn = _tile(n, 2048)
    tk = _tile(k, 2048)
    nj, nk = n // tn, k // tk
    steps = (m // tm) * nj * nk
    out_b = jnp.dtype(out_dtype).itemsize
    vmem = 2 * (tm * tk * 2 + tk * tn * 2 + tm * tn * out_b) + 2 * tm * tn * 4 + (4 << 20)
    in_specs = [pl.BlockSpec((tm, tk), lambda i, j, kk: (i, kk)),
                pl.BlockSpec((None, tk, tn), lambda i, j, kk: (layer, kk, j))]
    out_specs = [pl.BlockSpec((tm, tn), lambda i, j, kk: (i, j))]
    out_shape = [jax.ShapeDtypeStruct((m, n), out_dtype)]
    step = lambda i, j, kk: (i * nj + j) * nk + kk
    for wf in casts:
        _, r, c = wf.shape
        rows = r // steps
        assert rows * steps == r and rows % 16 == 0, (wf.shape, steps)
        in_specs.append(pl.BlockSpec((None, rows, c), lambda i, j, kk: (cast_layer, step(i, j, kk), 0)))
        out_specs.append(pl.BlockSpec((None, rows, c), lambda i, j, kk: (0, step(i, j, kk), 0)))
        out_shape.append(jax.ShapeDtypeStruct((1, r, c), BF16))
        vmem += 2 * rows * c * (4 + 2)
    outs = pl.pallas_call(
        functools.partial(_mm_acc_kernel, n_cast=len(casts)),
        grid=(m // tm, nj, nk),
        in_specs=in_specs,
        out_specs=out_specs,
        out_shape=out_shape,
        scratch_shapes=[pltpu.VMEM((tm, tn), F32)],
        compiler_params=_params(("arbitrary", "arbitrary", "arbitrary"), vmem),
        name="matmul_acc",
    )(a, w, *casts)
    return outs[0], outs[1:]


def _mm2_kernel(a1_ref, a2_ref, b1_ref, b2_ref, o_ref):
    acc = jnp.dot(a1_ref[...], b1_ref[...], preferred_element_type=F32)
    acc = acc + jnp.dot(a2_ref[...], b2_ref[...], preferred_element_type=F32)
    o_ref[...] = acc.astype(o_ref.dtype)


def _matmul_cat(a1, a2, w, layer, *, out_dtype=BF16):
    m, k1 = a1.shape
    k2 = a2.shape[1]
    assert k1 == k2
    n = w.shape[2]
    tm = _tile(m, 1024)
    tn = _tile(n, 1024)
    out_b = jnp.dtype(out_dtype).itemsize
    vmem = 2 * (2 * tm * k1 * 2 + 2 * k1 * tn * 2 + tm * tn * out_b) + 2 * tm * tn * 4 + (4 << 20)
    return pl.pallas_call(
        _mm2_kernel,
        grid=(m // tm, n // tn),
        in_specs=[pl.BlockSpec((tm, k1), lambda i, j: (i, 0)),
                  pl.BlockSpec((tm, k2), lambda i, j: (i, 0)),
                  pl.BlockSpec((None, k1, tn), lambda i, j: (layer, 0, j)),
                  pl.BlockSpec((None, k2, tn), lambda i, j: (layer, 1, j))],
        out_specs=pl.BlockSpec((tm, tn), lambda i, j: (i, j)),
        out_shape=jax.ShapeDtypeStruct((m, n), out_dtype),
        compiler_params=_params(("arbitrary", "arbitrary"), vmem),
        name="matmul_cat",
    )(a1, a2, w, w)


def _pool_kernel(u_ref, wp_ref, wmix_ref, scale_ref, o_ref, carry, *, per_seq, tm, gdim):
    i = pl.program_id(0)

    @pl.when((i % per_seq) == 0)
    def _():
        carry[...] = jnp.zeros_like(carry)

    t1 = ((i % per_seq) * tm + 1 + lax.broadcasted_iota(jnp.int32, (tm, 1), 0)).astype(F32)
    u = u_ref[...]
    for g, w in enumerate(POOL_WINDOWS):
        cols = slice(g * gdim, (g + 1) * gdim)
        p = jnp.dot(u, wp_ref[:, cols], preferred_element_type=F32)
        win = jnp.concatenate([carry[:, cols], p], axis=0)
        carry[:, cols] = p[tm - POOL_HALO:, :]
        span = 1
        while span < w:
            win = win + pltpu.roll(win, span, 0)
            span *= 2
        pooled = win[POOL_HALO:, :] / jnp.minimum(t1, float(w)) - p
        mixed = jnp.dot(pooled.astype(BF16), wmix_ref[g], preferred_element_type=F32)
        o_ref[:, cols] = (mixed * scale_ref[:, cols]).astype(o_ref.dtype)


def _pool_branch(u, w_in, w_pool, pool_scale, layer, seq):
    m, d = u.shape
    groups = len(POOL_WINDOWS)
    gdim = w_pool.shape[-1]
    pw = groups * gdim
    p_block = w_in.shape[2] // pw - 1
    tm = _tile(seq, 512)
    per_seq = seq // tm
    once = pl.Buffered(1)
    vmem = (2 * tm * d * 2 + d * pw * 2 + groups * gdim * gdim * 2 + 2 * tm * pw * 2
            + POOL_HALO * pw * 4 + 8 * (tm + POOL_HALO) * gdim * 4 + (4 << 20))
    return pl.pallas_call(
        functools.partial(_pool_kernel, per_seq=per_seq, tm=tm, gdim=gdim),
        grid=(m // tm,),
        in_specs=[pl.BlockSpec((tm, d), lambda i: (i, 0)),
                  pl.BlockSpec((None, d, pw), lambda i: (0, 0, p_block), pipeline_mode=once),
                  pl.BlockSpec((None, groups, gdim, gdim), lambda i: (layer, 0, 0, 0), pipeline_mode=once),
                  pl.BlockSpec((None, 1, pw), lambda i: (layer, 0, 0), pipeline_mode=once)],
        out_specs=pl.BlockSpec((tm, pw), lambda i: (i, 0)),
        out_shape=jax.ShapeDtypeStruct((m, pw), BF16),
        scratch_shapes=[pltpu.VMEM((POOL_HALO, pw), F32)],
        compiler_params=_params(("arbitrary",), vmem),
        name="pool_branch",
    )(u, w_in, w_pool, pool_scale.reshape(pool_scale.shape[0], 1, pw))


def _ret_kernel(q_ref, k_ref, v_ref, g_ref, cq_ref, sq_ref, ck_ref, sk_ref, o_ref, state_ref, tab_ref,
                *, heads, n_chunks):
    c = CHUNK

    @pl.when((pl.program_id(0) == 0) & (pl.program_id(1) == 0))
    def _():
        row = lax.broadcasted_iota(jnp.int32, (c, c), 0)
        col = lax.broadcasted_iota(jnp.int32, (c, c), 1)
        diff = (row - col).astype(F32)
        ridx = row.astype(F32)
        for h in range(heads):
            lg = jnp.log(jnp.full((c, c), 1.0 - 2.0 ** (-5.0 - h), F32))
            tab_ref[h, 0] = jnp.where(diff >= 0, jnp.exp(jnp.maximum(diff, 0.0) * lg), 0.0)
            tab_ref[h, 1] = jnp.exp((c - 1.0 - ridx) * lg)
            tab_ref[h, 2] = jnp.exp((ridx + 1.0) * lg)

    @pl.when(pl.program_id(1) == 0)
    def _():
        state_ref[...] = jnp.zeros_like(state_ref)

    for h in range(heads):
        cdecay = jnp.exp(c * jnp.log(jnp.full((1, V_DIM), 1.0 - 2.0 ** (-5.0 - h), F32)))
        qs = slice(h * QK_DIM, (h + 1) * QK_DIM)
        vs = slice(h * V_DIM, (h + 1) * V_DIM)
        for n in range(n_chunks):
            rows = slice(n * c, (n + 1) * c)
            q = q_ref[rows, qs].astype(F32)
            k = k_ref[rows, qs].astype(F32)
            qr = q * cq_ref[rows, :] + pltpu.roll(q, QK_DIM // 2, 1) * sq_ref[rows, :]
            kr = k * ck_ref[rows, :] + pltpu.roll(k, QK_DIM // 2, 1) * sk_ref[rows, :]
            v = v_ref[rows, vs]
            scores = lax.dot_general(qr.astype(BF16), kr.astype(BF16), (((1,), (1,)), ((), ())),
                                     preferred_element_type=F32) * tab_ref[h, 0]
            kv = jnp.dot((kr * tab_ref[h, 1]).astype(BF16).T, v, preferred_element_type=F32)
            state = state_ref[h]
            lhs = jnp.concatenate([scores.astype(BF16), (qr * tab_ref[h, 2]).astype(BF16)], axis=1)
            rhs = jnp.concatenate([v, state.astype(BF16)], axis=0)
            out = jnp.dot(lhs, rhs, preferred_element_type=F32)
            state_ref[h] = state * cdecay + kv
            gate = jax.nn.silu(g_ref[rows, vs].astype(F32))
            o_ref[rows, vs] = (gate * _ln(out)).astype(o_ref.dtype)


def _retention(h_qkvg, tables, batch, seq):
    m, width = h_qkvg.shape
    rw = width // 3
    heads = rw // V_DIM
    qkw = heads * QK_DIM
    assert 2 * qkw == rw and CHUNK == QK_DIM
    tr = _tile(seq, 4 * CHUNK)
    per_seq = seq // tr
    rowblk = lambda b, s: b * per_seq + s
    tab = pl.BlockSpec((tr, QK_DIM), lambda b, s: (rowblk(b, s), 0))
    vmem = (2 * tr * (2 * qkw * 2 + 3 * rw * 2 + 4 * QK_DIM * 4) + heads * QK_DIM * V_DIM * 4
            + heads * 3 * CHUNK * CHUNK * 4 + (16 << 20))
    return pl.pallas_call(
        functools.partial(_ret_kernel, heads=heads, n_chunks=tr // CHUNK),
        grid=(batch, per_seq),
        in_specs=[pl.BlockSpec((tr, qkw), lambda b, s: (rowblk(b, s), 0)),
                  pl.BlockSpec((tr, qkw), lambda b, s: (rowblk(b, s), 1)),
                  pl.BlockSpec((tr, rw), lambda b, s: (rowblk(b, s), 1)),
                  pl.BlockSpec((tr, rw), lambda b, s: (rowblk(b, s), 2)),
                  tab, tab, tab, tab],
        out_specs=pl.BlockSpec((tr, rw), lambda b, s: (rowblk(b, s), 0)),
        out_shape=jax.ShapeDtypeStruct((m, rw), BF16),
        scratch_shapes=[pltpu.VMEM((heads, QK_DIM, V_DIM), F32),
                        pltpu.VMEM((heads, 3, CHUNK, CHUNK), F32)],
        compiler_params=_params(("arbitrary", "arbitrary"), vmem),
        name="retention",
    )(h_qkvg, h_qkvg, h_qkvg, h_qkvg, *tables)


def _resid_kernel(x_ref, y_ref, gate_ref, gain_ref, bias_ref, xo_ref, *, alpha):
    xo_ref[...] = _resid_math(x_ref[...], y_ref[...], gate_ref[...], gain_ref[...], bias_ref[...], alpha)


def _resid_norm(x, y, gate, gain, bias, seq, alpha):
    m, d = x.shape
    tm = _tile(seq, 256)
    per_seq = seq // tm
    row = pl.BlockSpec((tm, d), lambda i: (i, 0))
    per_batch = pl.BlockSpec((None, 1, d), lambda i: (i // per_seq, 0, 0))
    shared = pl.BlockSpec((1, d), lambda i: (0, 0))
    vmem = 2 * tm * d * (4 + y.dtype.itemsize + 4) + 6 * tm * d * 4 + (4 << 20)
    return pl.pallas_call(
        functools.partial(_resid_kernel, alpha=alpha),
        grid=(m // tm,),
        in_specs=[row, row, per_batch, shared, shared],
        out_specs=row,
        out_shape=jax.ShapeDtypeStruct((m, d), F32),
        compiler_params=_params(("arbitrary",), vmem),
        name="resid_norm",
    )(x, y, gate, gain.reshape(1, d), bias.reshape(1, d))


def kernel(x, c, positions, w_mod, mod_table, w_in, w_pool, pool_scale, w_out, ln_mix_g, ln_mix_b,
           w_ff1, w_ff2, ln_ff_g, ln_ff_b):
    batch, seq, d = x.shape
    depth = w_in.shape[0]
    m = batch * seq
    alpha = (2 * depth) ** 0.25
    rw = d // 2
    assert seq % CHUNK == 0 and rw % V_DIM == 0 and w_in.shape[2] == 4 * rw

    mods = _modvec(c, w_mod, mod_table).reshape(depth, SUBLANES, N_MOD, d)

    def mod(layer, which):
        return mods[layer, :batch, which][:, None, :]

    tables = _rope_tables(positions)
    big_weights = [w_in, w_out, w_ff1, w_ff2]
    w_in_b, w_out_b, w_ff1_b, w_ff2_b = [w[:1].astype(BF16) for w in big_weights]
    w_pool_b = w_pool.astype(BF16)

    xf = x.reshape(m, d)
    u = _ln_mod(xf, mod(0, 1), mod(0, 0), seq)
    h_qkvg = _matmul(u, w_in_b, 0, 3 * rw)
    for l in range(depth):
        ret = _retention(h_qkvg, tables, batch, seq)
        pool = _pool_branch(u, w_in_b, w_pool_b, pool_scale, l, seq)
        y = _matmul_cat(ret, pool, w_out_b, 0)
        xf, hdn = _norm_matmul(xf, y, mod(l, 2), ln_mix_g[l], ln_mix_b[l], mod(l, 4), mod(l, 3), w_ff1_b,
                               w_ff1.shape[2], seq, alpha, relu2=True, emit_u=False)
        if l + 1 < depth:
            y, (w_in_b, w_out_b, w_ff1_b, w_ff2_b) = _matmul_acc(hdn, w_ff2_b, 0, big_weights, l + 1)
            xf, h_qkvg, u = _norm_matmul(xf, y, mod(l, 5), ln_ff_g[l], ln_ff_b[l], mod(l + 1, 1), mod(l + 1, 0),
                                         w_in_b, 3 * rw, seq, alpha, relu2=False, emit_u=True)
        else:
            y, _ = _matmul_acc(hdn, w_ff2_b, 0)
            xf = _resid_norm(xf, y, mod(l, 5), ln_ff_g[l], ln_ff_b[l], seq, alpha)
    return xf.reshape(batch, seq, d)
```

```python
import functools

import jax
import jax.numpy as jnp
from jax import lax
from jax.experimental import pallas as pl
from jax.experimental.pallas import tpu as pltpu

N_MOD = 6
QK_DIM = 128
V_DIM = 256
CHUNK = 128
POOL_WINDOWS = (2, 4, 8, 16)
POOL_HALO = 16
ROPE_BASE = 10000.0
LN_EPS = 1e-5
SUBLANES = 8
V7X_VMEM_CAP = 60000 * 1024
V7X_MXU_WIDTH = 256

F32 = jnp.float32
BF16 = jnp.bfloat16


def _tile(dim, pref):
    t = min(dim, pref)
    while dim % t:
        t //= 2
    return t


def _params(semantics, vmem_bytes):
    return pltpu.CompilerParams(dimension_semantics=semantics,
                                vmem_limit_bytes=int(min(vmem_bytes, V7X_VMEM_CAP)))


def _ln(x):
    mu = jnp.mean(x, axis=-1, keepdims=True)
    xc = x - mu
    var = jnp.mean(xc * xc, axis=-1, keepdims=True)
    return xc * lax.rsqrt(var + LN_EPS)


def _modvec_kernel(c_ref, w_ref, tab_ref, o_ref):
    s = jax.nn.silu(c_ref[...])
    base = jnp.dot(s.astype(BF16), w_ref[...].astype(BF16), preferred_element_type=F32)
    o_ref[...] = base[None, :, :] + tab_ref[...]


def _modvec(c, w_mod, mod_table):
    b, d = c.shape
    depth = mod_table.shape[0]
    n = w_mod.shape[1]
    tn = _tile(n, 1024)
    c8 = jnp.zeros((SUBLANES, d), F32).at[:b].set(c)
    tab = mod_table.reshape(depth, 1, n)
    return pl.pallas_call(
        _modvec_kernel,
        grid=(n // tn,),
        in_specs=[pl.BlockSpec((SUBLANES, d), lambda j: (0, 0)),
                  pl.BlockSpec((d, tn), lambda j: (0, j)),
                  pl.BlockSpec((depth, 1, tn), lambda j: (0, 0, j))],
        out_specs=pl.BlockSpec((depth, SUBLANES, tn), lambda j: (0, 0, j)),
        out_shape=jax.ShapeDtypeStruct((depth, SUBLANES, n), F32),
        compiler_params=_params(("arbitrary",), 2 * d * tn * 4 + d * tn * 2 + (8 << 20)),
        name="modvec",
    )(c8, w_mod, tab)


def _rope_kernel(pos_ref, freq_ref, sign_ref, cq_ref, sq_ref, ck_ref, sk_ref):
    ang = pos_ref[...].astype(F32) * freq_ref[...]
    cos = jnp.cos(ang)
    sin = jnp.sin(ang) * sign_ref[...]
    scale = QK_DIM ** -0.5
    cq_ref[...] = cos
    sq_ref[...] = sin
    ck_ref[...] = cos * scale
    sk_ref[...] = sin * scale


def _rope_tables(positions):
    m = positions.size
    half = QK_DIM // 2
    inv_freq = ROPE_BASE ** (-jnp.arange(0, QK_DIM, 2, dtype=F32) / QK_DIM)
    freq = jnp.concatenate([inv_freq, inv_freq]).reshape(1, QK_DIM)
    sign = jnp.concatenate([-jnp.ones((half,), F32), jnp.ones((half,), F32)]).reshape(1, QK_DIM)
    tr = _tile(m, 1024)
    vec = pl.BlockSpec((1, QK_DIM), lambda i: (0, 0))
    tab = pl.BlockSpec((tr, QK_DIM), lambda i: (i, 0))
    return pl.pallas_call(
        _rope_kernel,
        grid=(m // tr,),
        in_specs=[pl.BlockSpec((tr, 1), lambda i: (i, 0)), vec, vec],
        out_specs=[tab] * 4,
        out_shape=[jax.ShapeDtypeStruct((m, QK_DIM), F32)] * 4,
        compiler_params=_params(("arbitrary",), 32 << 20),
        name="rope_tables",
    )(positions.reshape(m, 1), freq, sign)


def _ln_mod_kernel(x_ref, sc_ref, sh_ref, u_ref):
    u_ref[...] = (_ln(x_ref[...]) * (1.0 + sc_ref[...]) + sh_ref[...]).astype(u_ref.dtype)


def _ln_mod(x, sc, sh, seq):
    m, d = x.shape
    tm = _tile(seq, 512)
    per_seq = seq // tm
    vec = pl.BlockSpec((None, 1, d), lambda i: (i // per_seq, 0, 0))
    return pl.pallas_call(
        _ln_mod_kernel,
        grid=(m // tm,),
        in_specs=[pl.BlockSpec((tm, d), lambda i: (i, 0)), vec, vec],
        out_specs=pl.BlockSpec((tm, d), lambda i: (i, 0)),
        out_shape=jax.ShapeDtypeStruct((m, d), BF16),
        compiler_params=_params(("arbitrary",), 4 * tm * d * 4 + (8 << 20)),
        name="ln_mod",
    )(x, sc, sh)


def _cast_plan(casts, cast_layer, steps, step_index):
    slabs = 1 << (steps.bit_length() - 1)
    slab = lambda *ids: jnp.minimum(step_index(*ids), slabs - 1)
    in_specs, out_specs, out_shape, vmem = [], [], [], 0
    for wf in casts:
        _, r, c = wf.shape
        rows = r // slabs
        assert rows * slabs == r and rows % 16 == 0, (wf.shape, slabs)
        in_specs.append(pl.BlockSpec((None, rows, c), lambda *ids: (cast_layer, slab(*ids), 0)))
        out_specs.append(pl.BlockSpec((None, rows, c), lambda *ids: (0, slab(*ids), 0)))
        out_shape.append(jax.ShapeDtypeStruct((1, r, c), BF16))
        vmem += 2 * rows * c * (4 + 2)
    return in_specs, out_specs, out_shape, vmem


def _cast_slabs(srcs, dsts):
    for src, dst in zip(srcs, dsts):
        dst[...] = src[...].astype(dst.dtype)


def _mm_kernel(a_ref, b_ref, *refs, n_cast):
    o_ref = refs[n_cast]
    o_ref[...] = jnp.dot(a_ref[...], b_ref[...], preferred_element_type=F32).astype(o_ref.dtype)
    _cast_slabs(refs[:n_cast], refs[n_cast + 1:])


def _matmul(a, w, layer, n, casts=(), cast_layer=None, *, out_dtype=BF16):
    m, k = a.shape
    tm = _tile(m, 1024)
    tn = _tile(n, 1024)
    nj = n // tn
    out_b = jnp.dtype(out_dtype).itemsize
    cast_in, cast_out, cast_shape, cast_vmem = _cast_plan(casts, cast_layer, (m // tm) * nj,
                                                          lambda i, j: i * nj + j)
    vmem = 2 * (tm * k * 2 + k * tn * 2 + tm * tn * out_b) + 2 * tm * tn * 4 + (4 << 20) + cast_vmem
    outs = pl.pallas_call(
        functools.partial(_mm_kernel, n_cast=len(casts)),
        grid=(m // tm, nj),
        in_specs=[pl.BlockSpec((tm, k), lambda i, j: (i, 0)),
                  pl.BlockSpec((None, k, tn), lambda i, j: (layer, 0, j))] + cast_in,
        out_specs=[pl.BlockSpec((tm, tn), lambda i, j: (i, j))] + cast_out,
        out_shape=[jax.ShapeDtypeStruct((m, n), out_dtype)] + cast_shape,
        compiler_params=_params(("arbitrary", "arbitrary"), vmem),
        name="matmul",
    )(a, w, *casts)
    return outs[0], outs[1:]


def _resid_math(x, y, gate, gain, bias, alpha):
    return _ln(alpha * x + gate * y.astype(F32)) * gain + bias


def _norm_mm_kernel(x_ref, y_ref, gate_ref, gain_ref, bias_ref, sc_ref, sh_ref, w_ref, *refs,
                    alpha, rows, relu2, emit_u, n_cast):
    xo_ref, o_ref = refs[n_cast], refs[n_cast + 1]
    uo_ref = refs[n_cast + 2] if emit_u else None
    u0, u1 = refs[-2:]
    i = pl.program_id(0)
    j = pl.program_id(1)

    def norm_rows(u_next):
        xn = _resid_math(x_ref[...], y_ref[...], gate_ref[...], gain_ref[...], bias_ref[...], alpha)
        xo_ref[...] = xn
        u = (_ln(xn) * (1.0 + sc_ref[...]) + sh_ref[...]).astype(u_next.dtype)
        u_next[pl.ds(pl.multiple_of(j * rows, rows), rows), :] = u
        if emit_u:
            uo_ref[...] = u
        _cast_slabs(refs[:n_cast], refs[-2 - n_cast:-2])

    @pl.when(i == 0)
    def _():
        norm_rows(u0)

    for parity, (u_cur, u_next) in enumerate(((u1, u0), (u0, u1))):
        @pl.when((i > 0) & (i % 2 == parity))
        def _():
            acc = jnp.dot(u_cur[...], w_ref[...], preferred_element_type=F32)
            if relu2:
                acc = jnp.square(jnp.maximum(acc, 0.0))
            o_ref[...] = acc.astype(o_ref.dtype)
            norm_rows(u_next)


def _norm_matmul(x, y, gate, gain, bias, sc, sh, w, n, seq, alpha, casts=(), cast_layer=None, *, relu2, emit_u):
    m, d = x.shape
    tm = _tile(seq, 1024)
    nm = m // tm
    tn = next(t for t in range(1024, 0, -V7X_MXU_WIDTH)
              if n % t == 0 and tm % (n // t) == 0 and (tm // (n // t)) % 16 == 0)
    nj = n // tn
    rows = tm // nj
    per_seq = seq // tm
    slab = pl.BlockSpec((rows, d), lambda i, j: (jnp.where(i < nm, i * nj + j, nm * nj - 1), 0))
    per_batch = pl.BlockSpec((None, 1, d), lambda i, j: (jnp.minimum(i, nm - 1) // per_seq, 0, 0))
    shared = pl.BlockSpec((1, d), lambda i, j: (0, 0))
    col = lambda i, j: jnp.where(i > 0, j, 0)
    cast_in, cast_out, cast_shape, cast_vmem = _cast_plan(casts, cast_layer, (nm + 1) * nj,
                                                          lambda i, j: i * nj + j)
    vmem = (2 * tm * d * 2 + 2 * d * tn * 2 + 2 * tm * tn * 2 + 3 * tm * tn * 4
            + 2 * rows * d * (4 + 2 + 4 + 2) + 8 * rows * d * 4 + (4 << 20) + cast_vmem)
    out_specs = [slab, pl.BlockSpec((tm, tn), lambda i, j: (jnp.maximum(i - 1, 0), col(i, j)))]
    out_shape = [jax.ShapeDtypeStruct((m, d), F32), jax.ShapeDtypeStruct((m, n), BF16)]
    if emit_u:
        out_specs.append(slab)
        out_shape.append(jax.ShapeDtypeStruct((m, d), BF16))
    n_main = len(out_shape)
    outs = pl.pallas_call(
        functools.partial(_norm_mm_kernel, alpha=alpha, rows=rows, relu2=relu2, emit_u=emit_u,
                          n_cast=len(casts)),
        grid=(nm + 1, nj),
        in_specs=[slab, slab, per_batch, shared, shared, per_batch, per_batch,
                  pl.BlockSpec((None, d, tn), lambda i, j: (0, 0, col(i, j)))] + cast_in,
        out_specs=out_specs + cast_out,
        out_shape=out_shape + cast_shape,
        scratch_shapes=[pltpu.VMEM((tm, d), BF16), pltpu.VMEM((tm, d), BF16)],
        compiler_params=_params(("arbitrary", "arbitrary"), vmem),
        name="norm_matmul_relu2" if relu2 else "norm_matmul",
    )(x, y, gate, gain.reshape(1, d), bias.reshape(1, d), sc, sh, w, *casts)
    return outs[:n_main], outs[n_main:]


def _mm_acc_kernel(a_ref, b_ref, *refs, n_cast):
    o_ref, acc_ref = refs[n_cast], refs[-1]
    kk = pl.program_id(2)

    @pl.when(kk == 0)
    def _():
        acc_ref[...] = jnp.zeros_like(acc_ref)

    acc_ref[...] += jnp.dot(a_ref[...], b_ref[...], preferred_element_type=F32)
    _cast_slabs(refs[:n_cast], refs[n_cast + 1:-1])

    @pl.when(kk == pl.num_programs(2) - 1)
    def _():
        o_ref[...] = acc_ref[...].astype(o_ref.dtype)


def _matmul_acc(a, w, layer, casts=(), cast_layer=None, *, out_dtype=BF16):
    m, k = a.shape
    n = w.shape[2]
    tm = _tile(m, 1024)
    tn = _tile(n, 2048)
    tk = _tile(k, 2048)
    nj, nk = n // tn, k // tk
    out_b = jnp.dtype(out_dtype).itemsize
    cast_in, cast_out, cast_shape, cast_vmem = _cast_plan(casts, cast_layer, (m // tm) * nj * nk,
                                                          lambda i, j, kk: (i * nj + j) * nk + kk)
    vmem = 2 * (tm * tk * 2 + tk * tn * 2 + tm * tn * out_b) + 2 * tm * tn * 4 + (4 << 20) + cast_vmem
    outs = pl.pallas_call(
        functools.partial(_mm_acc_kernel, n_cast=len(casts)),
        grid=(m // tm, nj, nk),
        in_specs=[pl.BlockSpec((tm, tk), lambda i, j, kk: (i, kk)),
                  pl.BlockSpec((None, tk, tn), lambda i, j, kk: (layer, kk, j))] + cast_in,
        out_specs=[pl.BlockSpec((tm, tn), lambda i, j, kk: (i, j))] + cast_out,
        out_shape=[jax.ShapeDtypeStruct((m, n), out_dtype)] + cast_shape,
        scratch_shapes=[pltpu.VMEM((tm, tn), F32)],
        compiler_params=_params(("arbitrary", "arbitrary", "arbitrary"), vmem),
        name="matmul_acc",
    )(a, w, *casts)
    return outs[0], outs[1:]


def _mm2_kernel(a1_ref, a2_ref, b1_ref, b2_ref, o_ref):
    acc = jnp.dot(a1_ref[...], b1_ref[...], preferred_element_type=F32)
    acc = acc + jnp.dot(a2_ref[...], b2_ref[...], preferred_element_type=F32)
    o_ref[...] = acc.astype(o_ref.dtype)


def _matmul_cat(a1, a2, w, layer, *, out_dtype=BF16):
    m, k1 = a1.shape
    k2 = a2.shape[1]
    assert k1 == k2
    n = w.shape[2]
    tm = _tile(m, 1024)
    tn = _tile(n, 1024)
    out_b = jnp.dtype(out_dtype).itemsize
    vmem = 2 * (2 * tm * k1 * 2 + 2 * k1 * tn * 2 + tm * tn * out_b) + 2 * tm * tn * 4 + (4 << 20)
    return pl.pallas_call(
        _mm2_kernel,
        grid=(m // tm, n // tn),
        in_specs=[pl.BlockSpec((tm, k1), lambda i, j: (i, 0)),
                  pl.BlockSpec((tm, k2), lambda i, j: (i, 0)),
                  pl.BlockSpec((None, k1, tn), lambda i, j: (layer, 0, j)),
                  pl.BlockSpec((None, k2, tn), lambda i, j: (layer, 1, j))],
        out_specs=pl.BlockSpec((tm, tn), lambda i, j: (i, j)),
        out_shape=jax.ShapeDtypeStruct((m, n), out_dtype),
        compiler_params=_params(("arbitrary", "arbitrary"), vmem),
        name="matmul_cat",
    )(a1, a2, w, w)


def _pool_kernel(u_ref, wp_ref, wmix_ref, scale_ref, o_ref, carry, *, per_seq, tm, gdim):
    i = pl.program_id(0)

    @pl.when((i % per_seq) == 0)
    def _():
        carry[...] = jnp.zeros_like(carry)

    t1 = ((i % per_seq) * tm + 1 + lax.broadcasted_iota(jnp.int32, (tm, 1), 0)).astype(F32)
    u = u_ref[...]
    for g, w in enumerate(POOL_WINDOWS):
        cols = slice(g * gdim, (g + 1) * gdim)
        p = jnp.dot(u, wp_ref[:, cols], preferred_element_type=F32)
        win = jnp.concatenate([carry[:, cols], p], axis=0)
        carry[:, cols] = p[tm - POOL_HALO:, :]
        span = 1
        while span < w:
            win = win + pltpu.roll(win, span, 0)
            span *= 2
        pooled = win[POOL_HALO:, :] / jnp.minimum(t1, float(w)) - p
        mixed = jnp.dot(pooled.astype(BF16), wmix_ref[g], preferred_element_type=F32)
        o_ref[:, cols] = (mixed * scale_ref[:, cols]).astype(o_ref.dtype)


def _pool_branch(u, w_in, w_pool, pool_scale, layer, seq):
    m, d = u.shape
    groups = len(POOL_WINDOWS)
    gdim = w_pool.shape[-1]
    pw = groups * gdim
    p_block = w_in.shape[2] // pw - 1
    tm = _tile(seq, 512)
    per_seq = seq // tm
    once = pl.Buffered(1)
    vmem = (2 * tm * d * 2 + d * pw * 2 + groups * gdim * gdim * 2 + 2 * tm * pw * 2
            + POOL_HALO * pw * 4 + 8 * (tm + POOL_HALO) * gdim * 4 + (4 << 20))
    return pl.pallas_call(
        functools.partial(_pool_kernel, per_seq=per_seq, tm=tm, gdim=gdim),
        grid=(m // tm,),
        in_specs=[pl.BlockSpec((tm, d), lambda i: (i, 0)),
                  pl.BlockSpec((None, d, pw), lambda i: (0, 0, p_block), pipeline_mode=once),
                  pl.BlockSpec((None, groups, gdim, gdim), lambda i: (layer, 0, 0, 0), pipeline_mode=once),
                  pl.BlockSpec((None, 1, pw), lambda i: (layer, 0, 0), pipeline_mode=once)],
        out_specs=pl.BlockSpec((tm, pw), lambda i: (i, 0)),
        out_shape=jax.ShapeDtypeStruct((m, pw), BF16),
        scratch_shapes=[pltpu.VMEM((POOL_HALO, pw), F32)],
        compiler_params=_params(("arbitrary",), vmem),
        name="pool_branch",
    )(u, w_in, w_pool, pool_scale.reshape(pool_scale.shape[0], 1, pw))


def _ret_kernel(q_ref, k_ref, v_ref, g_ref, cq_ref, sq_ref, ck_ref, sk_ref, *refs, heads, n_chunks, n_cast):
    o_ref, state_ref, tab_ref = refs[n_cast], refs[-2], refs[-1]
    c = CHUNK

    @pl.when((pl.program_id(0) == 0) & (pl.program_id(1) == 0))
    def _():
        row = lax.broadcasted_iota(jnp.int32, (c, c), 0)
        col = lax.broadcasted_iota(jnp.int32, (c, c), 1)
        diff = (row - col).astype(F32)
        ridx = row.astype(F32)
        for h in range(heads):
            lg = jnp.log(jnp.full((c, c), 1.0 - 2.0 ** (-5.0 - h), F32))
            tab_ref[h, 0] = jnp.where(diff >= 0, jnp.exp(jnp.maximum(diff, 0.0) * lg), 0.0)
            tab_ref[h, 1] = jnp.exp((c - 1.0 - ridx) * lg)
            tab_ref[h, 2] = jnp.exp((ridx + 1.0) * lg)

    @pl.when(pl.program_id(1) == 0)
    def _():
        state_ref[...] = jnp.zeros_like(state_ref)

    def chunk(n, carry):
        rows = pl.ds(pl.multiple_of(n * c, c), c)
        cq, sq, ck, sk = cq_ref[rows, :], sq_ref[rows, :], ck_ref[rows, :], sk_ref[rows, :]
        for h in range(heads):
            cdecay = jnp.exp(c * jnp.log(jnp.full((1, V_DIM), 1.0 - 2.0 ** (-5.0 - h), F32)))
            qs = slice(h * QK_DIM, (h + 1) * QK_DIM)
            vs = slice(h * V_DIM, (h + 1) * V_DIM)
            q = q_ref[rows, qs].astype(F32)
            k = k_ref[rows, qs].astype(F32)
            qr = q * cq + pltpu.roll(q, QK_DIM // 2, 1) * sq
            kr = k * ck + pltpu.roll(k, QK_DIM // 2, 1) * sk
            v = v_ref[rows, vs]
            scores = lax.dot_general(qr.astype(BF16), kr.astype(BF16), (((1,), (1,)), ((), ())),
                                     preferred_element_type=F32) * tab_ref[h, 0]
            kv = jnp.dot((kr * tab_ref[h, 1]).astype(BF16).T, v, preferred_element_type=F32)
            state = state_ref[h]
            lhs = jnp.concatenate([scores.astype(BF16), (qr * tab_ref[h, 2]).astype(BF16)], axis=1)
            rhs = jnp.concatenate([v, state.astype(BF16)], axis=0)
            out = jnp.dot(lhs, rhs, preferred_element_type=F32)
            state_ref[h] = state * cdecay + kv
            gate = jax.nn.silu(g_ref[rows, vs].astype(F32))
            o_ref[rows, vs] = (gate * _ln(out)).astype(o_ref.dtype)
        return carry

    lax.fori_loop(0, n_chunks, chunk, 0)
    _cast_slabs(refs[:n_cast], refs[n_cast + 1:-2])


def _retention(h_qkvg, tables, batch, seq, casts=(), cast_layer=None):
    m, width = h_qkvg.shape
    rw = width // 3
    heads = rw // V_DIM
    qkw = heads * QK_DIM
    assert 2 * qkw == rw and CHUNK == QK_DIM
    tr = _tile(seq, 4 * CHUNK)
    per_seq = seq // tr
    rowblk = lambda b, s: b * per_seq + s
    tab = pl.BlockSpec((tr, QK_DIM), lambda b, s: (rowblk(b, s), 0))
    cast_in, cast_out, cast_shape, cast_vmem = _cast_plan(casts, cast_layer, batch * per_seq, rowblk)
    vmem = (2 * tr * (2 * qkw * 2 + 3 * rw * 2 + 4 * QK_DIM * 4) + heads * QK_DIM * V_DIM * 4
            + heads * 3 * CHUNK * CHUNK * 4 + (16 << 20) + cast_vmem)
    outs = pl.pallas_call(
        functools.partial(_ret_kernel, heads=heads, n_chunks=tr // CHUNK, n_cast=len(casts)),
        grid=(batch, per_seq),
        in_specs=[pl.BlockSpec((tr, qkw), lambda b, s: (rowblk(b, s), 0)),
                  pl.BlockSpec((tr, qkw), lambda b, s: (rowblk(b, s), 1)),
                  pl.BlockSpec((tr, rw), lambda b, s: (rowblk(b, s), 1)),
                  pl.BlockSpec((tr, rw), lambda b, s: (rowblk(b, s), 2)),
                  tab, tab, tab, tab] + cast_in,
        out_specs=[pl.BlockSpec((tr, rw), lambda b, s: (rowblk(b, s), 0))] + cast_out,
        out_shape=[jax.ShapeDtypeStruct((m, rw), BF16)] + cast_shape,
        scratch_shapes=[pltpu.VMEM((heads, QK_DIM, V_DIM), F32),
                        pltpu.VMEM((heads, 3, CHUNK, CHUNK), F32)],
        compiler_params=_params(("arbitrary", "arbitrary"), vmem),
        name="retention",
    )(h_qkvg, h_qkvg, h_qkvg, h_qkvg, *tables, *casts)
    return outs[0], outs[1:]


def _resid_kernel(x_ref, y_ref, gate_ref, gain_ref, bias_ref, xo_ref, *, alpha):
    xo_ref[...] = _resid_math(x_ref[...], y_ref[...], gate_ref[...], gain_ref[...], bias_ref[...], alpha)


def _resid_norm(x, y, gate, gain, bias, seq, alpha):
    m, d = x.shape
    tm = _tile(seq, 256)
    per_seq = seq // tm
    row = pl.BlockSpec((tm, d), lambda i: (i, 0))
    per_batch = pl.BlockSpec((None, 1, d), lambda i: (i // per_seq, 0, 0))
    shared = pl.BlockSpec((1, d), lambda i: (0, 0))
    vmem = 2 * tm * d * (4 + y.dtype.itemsize + 4) + 6 * tm * d * 4 + (4 << 20)
    return pl.pallas_call(
        functools.partial(_resid_kernel, alpha=alpha),
        grid=(m // tm,),
        in_specs=[row, row, per_batch, shared, shared],
        out_specs=row,
        out_shape=jax.ShapeDtypeStruct((m, d), F32),
        compiler_params=_params(("arbitrary",), vmem),
        name="resid_norm",
    )(x, y, gate, gain.reshape(1, d), bias.reshape(1, d))


def kernel(x, c, positions, w_mod, mod_table, w_in, w_pool, pool_scale, w_out, ln_mix_g, ln_mix_b,
           w_ff1, w_ff2, ln_ff_g, ln_ff_b):
    batch, seq, d = x.shape
    depth = w_in.shape[0]
    m = batch * seq
    alpha = (2 * depth) ** 0.25
    rw = d // 2
    assert seq % CHUNK == 0 and rw % V_DIM == 0 and w_in.shape[2] == 4 * rw

    mods = _modvec(c, w_mod, mod_table).reshape(depth, SUBLANES, N_MOD, d)

    def mod(layer, which):
        return mods[layer, :batch, which][:, None, :]

    tables = _rope_tables(positions)
    w_in_b = w_in[:1].astype(BF16)
    w_pool_b = w_pool.astype(BF16)

    xf = x.reshape(m, d)
    u = _ln_mod(xf, mod(0, 1), mod(0, 0), seq)
    h_qkvg, (w_out_b,) = _matmul(u, w_in_b, 0, 3 * rw, [w_out], 0)
    for l in range(depth):
        first, last = l == 0, l + 1 == depth
        ret, cast = _retention(h_qkvg, tables, batch, seq, [w_ff1] if first else [], 0)
        if first:
            (w_ff1_b,) = cast
        pool = _pool_branch(u, w_in_b, w_pool_b, pool_scale, l, seq)
        y = _matmul_cat(ret, pool, w_out_b, 0)
        (xf, hdn), cast = _norm_matmul(xf, y, mod(l, 2), ln_mix_g[l], ln_mix_b[l], mod(l, 4), mod(l, 3), w_ff1_b,
                                       w_ff1.shape[2], seq, alpha, [w_ff2] if first else [], 0,
                                       relu2=True, emit_u=False)
        if first:
            (w_ff2_b,) = cast
        if last:
            y, _ = _matmul_acc(hdn, w_ff2_b, 0)
            xf = _resid_norm(xf, y, mod(l, 5), ln_ff_g[l], ln_ff_b[l], seq, alpha)
        else:
            y, (w_in_b, w_out_b, w_ff1_b, w_ff2_b) = _matmul_acc(hdn, w_ff2_b, 0, [w_in, w_out, w_ff1, w_ff2], l + 1)
            (xf, h_qkvg, u), _ = _norm_matmul(xf, y, mod(l, 5), ln_ff_g[l], ln_ff_b[l], mod(l + 1, 1),
                                              mod(l + 1, 0), w_in_b, 3 * rw, seq, alpha, relu2=False, emit_u=True)
    return xf.reshape(batch, seq, d)
```

```python
import functools

import jax
import jax.numpy as jnp
from jax import lax
from jax.experimental import pallas as pl
from jax.experimental.pallas import tpu as pltpu

N_MOD = 6
QK_DIM = 128
V_DIM = 256
CHUNK = 128
POOL_WINDOWS = (2, 4, 8, 16)
POOL_HALO = 16
ROPE_BASE = 10000.0
LN_EPS = 1e-5
SUBLANES = 8
V7X_VMEM_CAP = 60000 * 1024
V7X_MXU_WIDTH = 256

F32 = jnp.float32
BF16 = jnp.bfloat16


def _tile(dim, pref):
    t = min(dim, pref)
    while dim % t:
        t //= 2
    return t


def _params(semantics, vmem_bytes):
    return pltpu.CompilerParams(dimension_semantics=semantics,
                                vmem_limit_bytes=int(min(vmem_bytes, V7X_VMEM_CAP)))


def _ln(x):
    mu = jnp.mean(x, axis=-1, keepdims=True)
    xc = x - mu
    var = jnp.mean(xc * xc, axis=-1, keepdims=True)
    return xc * lax.rsqrt(var + LN_EPS)


def _modvec_kernel(c_ref, w_ref, tab_ref, o_ref):
    s = jax.nn.silu(c_ref[...])
    base = jnp.dot(s.astype(BF16), w_ref[...].astype(BF16), preferred_element_type=F32)
    o_ref[...] = base[None, :, :] + tab_ref[...]


def _modvec(c, w_mod, mod_table):
    b, d = c.shape
    depth = mod_table.shape[0]
    n = w_mod.shape[1]
    tn = _tile(n, 1024)
    c8 = jnp.zeros((SUBLANES, d), F32).at[:b].set(c)
    tab = mod_table.reshape(depth, 1, n)
    return pl.pallas_call(
        _modvec_kernel,
        grid=(n // tn,),
        in_specs=[pl.BlockSpec((SUBLANES, d), lambda j: (0, 0)),
                  pl.BlockSpec((d, tn), lambda j: (0, j)),
                  pl.BlockSpec((depth, 1, tn), lambda j: (0, 0, j))],
        out_specs=pl.BlockSpec((depth, SUBLANES, tn), lambda j: (0, 0, j)),
        out_shape=jax.ShapeDtypeStruct((depth, SUBLANES, n), F32),
        compiler_params=_params(("arbitrary",), 2 * d * tn * 4 + d * tn * 2 + (8 << 20)),
        name="modvec",
    )(c8, w_mod, tab)


def _rope_kernel(pos_ref, freq_ref, sign_ref, cq_ref, sq_ref, ck_ref, sk_ref):
    ang = pos_ref[...].astype(F32) * freq_ref[...]
    cos = jnp.cos(ang)
    sin = jnp.sin(ang) * sign_ref[...]
    scale = QK_DIM ** -0.5
    cq_ref[...] = cos
    sq_ref[...] = sin
    ck_ref[...] = cos * scale
    sk_ref[...] = sin * scale


def _rope_tables(positions):
    m = positions.size
    half = QK_DIM // 2
    inv_freq = ROPE_BASE ** (-jnp.arange(0, QK_DIM, 2, dtype=F32) / QK_DIM)
    freq = jnp.concatenate([inv_freq, inv_freq]).reshape(1, QK_DIM)
    sign = jnp.concatenate([-jnp.ones((half,), F32), jnp.ones((half,), F32)]).reshape(1, QK_DIM)
    tr = _tile(m, 1024)
    vec = pl.BlockSpec((1, QK_DIM), lambda i: (0, 0))
    tab = pl.BlockSpec((tr, QK_DIM), lambda i: (i, 0))
    return pl.pallas_call(
        _rope_kernel,
        grid=(m // tr,),
        in_specs=[pl.BlockSpec((tr, 1), lambda i: (i, 0)), vec, vec],
        out_specs=[tab] * 4,
        out_shape=[jax.ShapeDtypeStruct((m, QK_DIM), F32)] * 4,
        compiler_params=_params(("arbitrary",), 32 << 20),
        name="rope_tables",
    )(positions.reshape(m, 1), freq, sign)


def _ln_mod_kernel(x_ref, sc_ref, sh_ref, u_ref):
    u_ref[...] = (_ln(x_ref[...]) * (1.0 + sc_ref[...]) + sh_ref[...]).astype(u_ref.dtype)


def _ln_mod(x, sc, sh, seq):
    m, d = x.shape
    tm = _tile(seq, 512)
    per_seq = seq // tm
    vec = pl.BlockSpec((None, 1, d), lambda i: (i // per_seq, 0, 0))
    return pl.pallas_call(
        _ln_mod_kernel,
        grid=(m // tm,),
        in_specs=[pl.BlockSpec((tm, d), lambda i: (i, 0)), vec, vec],
        out_specs=pl.BlockSpec((tm, d), lambda i: (i, 0)),
        out_shape=jax.ShapeDtypeStruct((m, d), BF16),
        compiler_params=_params(("arbitrary",), 4 * tm * d * 4 + (8 << 20)),
        name="ln_mod",
    )(x, sc, sh)


def _cast_plan(casts, steps, step_index):
    slabs = 1 << (steps.bit_length() - 1)
    slab = lambda *ids: jnp.minimum(step_index(*ids), slabs - 1)
    in_specs, out_specs, out_shape, vmem = [], [], [], 0
    for wf, cast_layer in casts:
        _, r, c = wf.shape
        rows = r // slabs
        assert rows * slabs == r and rows % 16 == 0, (wf.shape, slabs)
        src_map = lambda *ids, layer=cast_layer: (layer, slab(*ids), 0)
        in_specs.append(pl.BlockSpec((None, rows, c), src_map))
        out_specs.append(pl.BlockSpec((None, rows, c), lambda *ids: (0, slab(*ids), 0)))
        out_shape.append(jax.ShapeDtypeStruct((1, r, c), BF16))
        vmem += 2 * rows * c * (4 + 2)
    return in_specs, out_specs, out_shape, vmem


def _cast_slabs(srcs, dsts):
    for src, dst in zip(srcs, dsts):
        dst[...] = src[...].astype(dst.dtype)


def _mm_kernel(a_ref, b_ref, *refs, n_cast):
    o_ref = refs[n_cast]
    o_ref[...] = jnp.dot(a_ref[...], b_ref[...], preferred_element_type=F32).astype(o_ref.dtype)
    _cast_slabs(refs[:n_cast], refs[n_cast + 1:])


def _matmul(a, w, layer, n, casts=(), *, out_dtype=BF16):
    m, k = a.shape
    tm = _tile(m, 1024)
    tn = _tile(n, 1024)
    nj = n // tn
    out_b = jnp.dtype(out_dtype).itemsize
    cast_in, cast_out, cast_shape, cast_vmem = _cast_plan(casts, (m // tm) * nj,
                                                          lambda i, j: i * nj + j)
    vmem = 2 * (tm * k * 2 + k * tn * 2 + tm * tn * out_b) + 2 * tm * tn * 4 + (4 << 20) + cast_vmem
    outs = pl.pallas_call(
        functools.partial(_mm_kernel, n_cast=len(casts)),
        grid=(m // tm, nj),
        in_specs=[pl.BlockSpec((tm, k), lambda i, j: (i, 0)),
                  pl.BlockSpec((None, k, tn), lambda i, j: (layer, 0, j))] + cast_in,
        out_specs=[pl.BlockSpec((tm, tn), lambda i, j: (i, j))] + cast_out,
        out_shape=[jax.ShapeDtypeStruct((m, n), out_dtype)] + cast_shape,
        compiler_params=_params(("arbitrary", "arbitrary"), vmem),
        name="matmul",
    )(a, w, *[wf for wf, _ in casts])
    return outs[0], outs[1:]


def _resid_math(x, y, gate, gain, bias, alpha):
    return _ln(alpha * x + gate * y.astype(F32)) * gain + bias


def _norm_mm_kernel(x_ref, y_ref, gate_ref, gain_ref, bias_ref, sc_ref, sh_ref, w_ref, *refs,
                    alpha, rows, relu2, emit_u, n_cast):
    xo_ref, o_ref = refs[n_cast], refs[n_cast + 1]
    uo_ref = refs[n_cast + 2] if emit_u else None
    u0, u1 = refs[-2:]
    i = pl.program_id(0)
    j = pl.program_id(1)

    def norm_rows(u_next):
        xn = _resid_math(x_ref[...], y_ref[...], gate_ref[...], gain_ref[...], bias_ref[...], alpha)
        xo_ref[...] = xn
        u = (_ln(xn) * (1.0 + sc_ref[...]) + sh_ref[...]).astype(u_next.dtype)
        u_next[pl.ds(pl.multiple_of(j * rows, rows), rows), :] = u
        if emit_u:
            uo_ref[...] = u
        _cast_slabs(refs[:n_cast], refs[-2 - n_cast:-2])

    @pl.when(i == 0)
    def _():
        norm_rows(u0)

    for parity, (u_cur, u_next) in enumerate(((u1, u0), (u0, u1))):
        @pl.when((i > 0) & (i % 2 == parity))
        def _():
            acc = jnp.dot(u_cur[...], w_ref[...], preferred_element_type=F32)
            if relu2:
                acc = jnp.square(jnp.maximum(acc, 0.0))
            o_ref[...] = acc.astype(o_ref.dtype)
            norm_rows(u_next)


def _norm_matmul(x, y, gate, gain, bias, sc, sh, w, n, seq, alpha, casts=(), *, relu2, emit_u):
    m, d = x.shape
    tm = _tile(seq, 1024)
    nm = m // tm
    tn = next(t for t in range(1024, 0, -V7X_MXU_WIDTH)
              if n % t == 0 and tm % (n // t) == 0 and (tm // (n // t)) % 16 == 0)
    nj = n // tn
    rows = tm // nj
    per_seq = seq // tm
    slab = pl.BlockSpec((rows, d), lambda i, j: (jnp.where(i < nm, i * nj + j, nm * nj - 1), 0))
    per_batch = pl.BlockSpec((None, 1, d), lambda i, j: (jnp.minimum(i, nm - 1) // per_seq, 0, 0))
    shared = pl.BlockSpec((1, d), lambda i, j: (0, 0))
    col = lambda i, j: jnp.where(i > 0, j, 0)
    cast_in, cast_out, cast_shape, cast_vmem = _cast_plan(casts, (nm + 1) * nj,
                                                          lambda i, j: i * nj + j)
    vmem = (2 * tm * d * 2 + 2 * d * tn * 2 + 2 * tm * tn * 2 + 3 * tm * tn * 4
            + 2 * rows * d * (4 + 2 + 4 + 2) + 8 * rows * d * 4 + (4 << 20) + cast_vmem)
    out_specs = [slab, pl.BlockSpec((tm, tn), lambda i, j: (jnp.maximum(i - 1, 0), col(i, j)))]
    out_shape = [jax.ShapeDtypeStruct((m, d), F32), jax.ShapeDtypeStruct((m, n), BF16)]
    if emit_u:
        out_specs.append(slab)
        out_shape.append(jax.ShapeDtypeStruct((m, d), BF16))
    n_main = len(out_shape)
    outs = pl.pallas_call(
        functools.partial(_norm_mm_kernel, alpha=alpha, rows=rows, relu2=relu2, emit_u=emit_u,
                          n_cast=len(casts)),
        grid=(nm + 1, nj),
        in_specs=[slab, slab, per_batch, shared, shared, per_batch, per_batch,
                  pl.BlockSpec((None, d, tn), lambda i, j: (0, 0, col(i, j)))] + cast_in,
        out_specs=out_specs + cast_out,
        out_shape=out_shape + cast_shape,
        scratch_shapes=[pltpu.VMEM((tm, d), BF16), pltpu.VMEM((tm, d), BF16)],
        compiler_params=_params(("arbitrary", "arbitrary"), vmem),
        name="norm_matmul_relu2" if relu2 else "norm_matmul",
    )(x, y, gate, gain.reshape(1, d), bias.reshape(1, d), sc, sh, w, *[wf for wf, _ in casts])
    return outs[:n_main], outs[n_main:]


def _mm_acc_kernel(a_ref, b_ref, o_ref, acc_ref):
    kk = pl.program_id(2)

    @pl.when(kk == 0)
    def _():
        acc_ref[...] = jnp.zeros_like(acc_ref)

    acc_ref[...] += jnp.dot(a_ref[...], b_ref[...], preferred_element_type=F32)

    @pl.when(kk == pl.num_programs(2) - 1)
    def _():
        o_ref[...] = acc_ref[...].astype(o_ref.dtype)


def _matmul_acc(a, w, layer, *, out_dtype=BF16):
    m, k = a.shape
    n = w.shape[2]
    tm = _tile(m, 1024)
    tn = _tile(n, 1024)
    tk = _tile(k, 4096)
    out_b = jnp.dtype(out_dtype).itemsize
    vmem = 2 * (tm * tk * 2 + tk * tn * 2 + tm * tn * out_b) + 2 * tm * tn * 4 + (4 << 20)
    return pl.pallas_call(
        _mm_acc_kernel,
        grid=(m // tm, n // tn, k // tk),
        in_specs=[pl.BlockSpec((tm, tk), lambda i, j, kk: (i, kk)),
                  pl.BlockSpec((None, tk, tn), lambda i, j, kk: (layer, kk, j))],
        out_specs=pl.BlockSpec((tm, tn), lambda i, j, kk: (i, j)),
        out_shape=jax.ShapeDtypeStruct((m, n), out_dtype),
        scratch_shapes=[pltpu.VMEM((tm, tn), F32)],
        compiler_params=_params(("arbitrary", "arbitrary", "arbitrary"), vmem),
        name="matmul_acc",
    )(a, w)


def _mm2_kernel(a1_ref, a2_ref, b1_ref, b2_ref, o_ref):
    acc = jnp.dot(a1_ref[...], b1_ref[...], preferred_element_type=F32)
    acc = acc + jnp.dot(a2_ref[...], b2_ref[...], preferred_element_type=F32)
    o_ref[...] = acc.astype(o_ref.dtype)


def _matmul_cat(a1, a2, w, layer, *, out_dtype=BF16):
    m, k1 = a1.shape
    k2 = a2.shape[1]
    assert k1 == k2
    n = w.shape[2]
    tm = _tile(m, 1024)
    tn = _tile(n, 1024)
    out_b = jnp.dtype(out_dtype).itemsize
    vmem = 2 * (2 * tm * k1 * 2 + 2 * k1 * tn * 2 + tm * tn * out_b) + 2 * tm * tn * 4 + (4 << 20)
    return pl.pallas_call(
        _mm2_kernel,
        grid=(m // tm, n // tn),
        in_specs=[pl.BlockSpec((tm, k1), lambda i, j: (i, 0)),
                  pl.BlockSpec((tm, k2), lambda i, j: (i, 0)),
                  pl.BlockSpec((None, k1, tn), lambda i, j: (layer, 0, j)),
                  pl.BlockSpec((None, k2, tn), lambda i, j: (layer, 1, j))],
        out_specs=pl.BlockSpec((tm, tn), lambda i, j: (i, j)),
        out_shape=jax.ShapeDtypeStruct((m, n), out_dtype),
        compiler_params=_params(("arbitrary", "arbitrary"), vmem),
        name="matmul_cat",
    )(a1, a2, w, w)


def _pool_kernel(u_ref, wp_ref, wmix_ref, scale_ref, o_ref, carry, *, per_seq, tm, gdim):
    i = pl.program_id(0)

    @pl.when((i % per_seq) == 0)
    def _():
        carry[...] = jnp.zeros_like(carry)

    t1 = ((i % per_seq) * tm + 1 + lax.broadcasted_iota(jnp.int32, (tm, 1), 0)).astype(F32)
    u = u_ref[...]
    for g, w in enumerate(POOL_WINDOWS):
        cols = slice(g * gdim, (g + 1) * gdim)
        p = jnp.dot(u, wp_ref[:, cols], preferred_element_type=F32)
        win = jnp.concatenate([carry[:, cols], p], axis=0)
        carry[:, cols] = p[tm - POOL_HALO:, :]
        span = 1
        while span < w:
            win = win + pltpu.roll(win, span, 0)
            span *= 2
        pooled = win[POOL_HALO:, :] / jnp.minimum(t1, float(w)) - p
        mixed = jnp.dot(pooled.astype(BF16), wmix_ref[g], preferred_element_type=F32)
        o_ref[:, cols] = (mixed * scale_ref[:, cols]).astype(o_ref.dtype)


def _pool_branch(u, w_in, w_pool, pool_scale, layer, seq):
    m, d = u.shape
    groups = len(POOL_WINDOWS)
    gdim = w_pool.shape[-1]
    pw = groups * gdim
    p_block = w_in.shape[2] // pw - 1
    tm = _tile(seq, 512)
    per_seq = seq // tm
    once = pl.Buffered(1)
    vmem = (2 * tm * d * 2 + d * pw * 2 + groups * gdim * gdim * 2 + 2 * tm * pw * 2
            + POOL_HALO * pw * 4 + 8 * (tm + POOL_HALO) * gdim * 4 + (4 << 20))
    return pl.pallas_call(
        functools.partial(_pool_kernel, per_seq=per_seq, tm=tm, gdim=gdim),
        grid=(m // tm,),
        in_specs=[pl.BlockSpec((tm, d), lambda i: (i, 0)),
                  pl.BlockSpec((None, d, pw), lambda i: (0, 0, p_block), pipeline_mode=once),
                  pl.BlockSpec((None, groups, gdim, gdim), lambda i: (layer, 0, 0, 0), pipeline_mode=once),
                  pl.BlockSpec((None, 1, pw), lambda i: (layer, 0, 0), pipeline_mode=once)],
        out_specs=pl.BlockSpec((tm, pw), lambda i: (i, 0)),
        out_shape=jax.ShapeDtypeStruct((m, pw), BF16),
        scratch_shapes=[pltpu.VMEM((POOL_HALO, pw), F32)],
        compiler_params=_params(("arbitrary",), vmem),
        name="pool_branch",
    )(u, w_in, w_pool, pool_scale.reshape(pool_scale.shape[0], 1, pw))


def _ret_kernel(q_ref, k_ref, v_ref, g_ref, cq_ref, sq_ref, ck_ref, sk_ref, *refs, heads, n_chunks, n_cast):
    o_ref, state_ref, tab_ref = refs[n_cast], refs[-2], refs[-1]
    c = CHUNK

    @pl.when((pl.program_id(0) == 0) & (pl.program_id(1) == 0))
    def _():
        row = lax.broadcasted_iota(jnp.int32, (c, c), 0)
        col = lax.broadcasted_iota(jnp.int32, (c, c), 1)
        diff = (row - col).astype(F32)
        ridx = row.astype(F32)
        for h in range(heads):
            lg = jnp.log(jnp.full((c, c), 1.0 - 2.0 ** (-5.0 - h), F32))
            tab_ref[h, 0] = jnp.where(diff >= 0, jnp.exp(jnp.maximum(diff, 0.0) * lg), 0.0)
            tab_ref[h, 1] = jnp.exp((c - 1.0 - ridx) * lg)
            tab_ref[h, 2] = jnp.exp((ridx + 1.0) * lg)

    @pl.when(pl.program_id(1) == 0)
    def _():
        state_ref[...] = jnp.zeros_like(state_ref)

    def chunk(n, carry):
        rows = pl.ds(pl.multiple_of(n * c, c), c)
        cq, sq, ck, sk = cq_ref[rows, :], sq_ref[rows, :], ck_ref[rows, :], sk_ref[rows, :]
        for h in range(heads):
            cdecay = jnp.exp(c * jnp.log(jnp.full((1, V_DIM), 1.0 - 2.0 ** (-5.0 - h), F32)))
            qs = slice(h * QK_DIM, (h + 1) * QK_DIM)
            vs = slice(h * V_DIM, (h + 1) * V_DIM)
            q = q_ref[rows, qs].astype(F32)
            k = k_ref[rows, qs].astype(F32)
            qr = q * cq + pltpu.roll(q, QK_DIM // 2, 1) * sq
            kr = k * ck + pltpu.roll(k, QK_DIM // 2, 1) * sk
            v = v_ref[rows, vs]
            scores = lax.dot_general(qr.astype(BF16), kr.astype(BF16), (((1,), (1,)), ((), ())),
                                     preferred_element_type=F32) * tab_ref[h, 0]
            kv = jnp.dot((kr * tab_ref[h, 1]).astype(BF16).T, v, preferred_element_type=F32)
            state = state_ref[h]
            lhs = jnp.concatenate([scores.astype(BF16), (qr * tab_ref[h, 2]).astype(BF16)], axis=1)
            rhs = jnp.concatenate([v, state.astype(BF16)], axis=0)
            out = jnp.dot(lhs, rhs, preferred_element_type=F32)
            state_ref[h] = state * cdecay + kv
            gate = jax.nn.silu(g_ref[rows, vs].astype(F32))
            o_ref[rows, vs] = (gate * _ln(out)).astype(o_ref.dtype)
        return carry

    lax.fori_loop(0, n_chunks, chunk, 0)
    _cast_slabs(refs[:n_cast], refs[n_cast + 1:-2])


def _retention(h_qkvg, tables, batch, seq, casts=()):
    m, width = h_qkvg.shape
    rw = width // 3
    heads = rw // V_DIM
    qkw = heads * QK_DIM
    assert 2 * qkw == rw and CHUNK == QK_DIM
    tr = _tile(seq, 4 * CHUNK)
    per_seq = seq // tr
    rowblk = lambda b, s: b * per_seq + s
    tab = pl.BlockSpec((tr, QK_DIM), lambda b, s: (rowblk(b, s), 0))
    cast_in, cast_out, cast_shape, cast_vmem = _cast_plan(casts, batch * per_seq, rowblk)
    vmem = (2 * tr * (2 * qkw * 2 + 3 * rw * 2 + 4 * QK_DIM * 4) + heads * QK_DIM * V_DIM * 4
            + heads * 3 * CHUNK * CHUNK * 4 + (16 << 20) + cast_vmem)
    outs = pl.pallas_call(
        functools.partial(_ret_kernel, heads=heads, n_chunks=tr // CHUNK, n_cast=len(casts)),
        grid=(batch, per_seq),
        in_specs=[pl.BlockSpec((tr, qkw), lambda b, s: (rowblk(b, s), 0)),
                  pl.BlockSpec((tr, qkw), lambda b, s: (rowblk(b, s), 1)),
                  pl.BlockSpec((tr, rw), lambda b, s: (rowblk(b, s), 1)),
                  pl.BlockSpec((tr, rw), lambda b, s: (rowblk(b, s), 2)),
                  tab, tab, tab, tab] + cast_in,
        out_specs=[pl.BlockSpec((tr, rw), lambda b, s: (rowblk(b, s), 0))] + cast_out,
        out_shape=[jax.ShapeDtypeStruct((m, rw), BF16)] + cast_shape,
        scratch_shapes=[pltpu.VMEM((heads, QK_DIM, V_DIM), F32),
                        pltpu.VMEM((heads, 3, CHUNK, CHUNK), F32)],
        compiler_params=_params(("arbitrary", "arbitrary"), vmem),
        name="retention",
    )(h_qkvg, h_qkvg, h_qkvg, h_qkvg, *tables, *[wf for wf, _ in casts])
    return outs[0], outs[1:]


def _resid_kernel(x_ref, y_ref, gate_ref, gain_ref, bias_ref, xo_ref, *, alpha):
    xo_ref[...] = _resid_math(x_ref[...], y_ref[...], gate_ref[...], gain_ref[...], bias_ref[...], alpha)


def _resid_norm(x, y, gate, gain, bias, seq, alpha):
    m, d = x.shape
    tm = _tile(seq, 256)
    per_seq = seq // tm
    row = pl.BlockSpec((tm, d), lambda i: (i, 0))
    per_batch = pl.BlockSpec((None, 1, d), lambda i: (i // per_seq, 0, 0))
    shared = pl.BlockSpec((1, d), lambda i: (0, 0))
    vmem = 2 * tm * d * (4 + y.dtype.itemsize + 4) + 6 * tm * d * 4 + (4 << 20)
    return pl.pallas_call(
        functools.partial(_resid_kernel, alpha=alpha),
        grid=(m // tm,),
        in_specs=[row, row, per_batch, shared, shared],
        out_specs=row,
        out_shape=jax.ShapeDtypeStruct((m, d), F32),
        compiler_params=_params(("arbitrary",), vmem),
        name="resid_norm",
    )(x, y, gate, gain.reshape(1, d), bias.reshape(1, d))


def kernel(x, c, positions, w_mod, mod_table, w_in, w_pool, pool_scale, w_out, ln_mix_g, ln_mix_b,
           w_ff1, w_ff2, ln_ff_g, ln_ff_b):
    batch, seq, d = x.shape
    depth = w_in.shape[0]
    m = batch * seq
    alpha = (2 * depth) ** 0.25
    rw = d // 2
    assert seq % CHUNK == 0 and rw % V_DIM == 0 and w_in.shape[2] == 4 * rw

    mods = _modvec(c, w_mod, mod_table).reshape(depth, SUBLANES, N_MOD, d)

    def mod(layer, which):
        return mods[layer, :batch, which][:, None, :]

    tables = _rope_tables(positions)
    w_in_b, w_ff2_b = w_in[:1].astype(BF16), w_ff2[:1].astype(BF16)
    w_pool_b = w_pool.astype(BF16)

    xf = x.reshape(m, d)
    u = _ln_mod(xf, mod(0, 1), mod(0, 0), seq)
    h_qkvg, (w_out_b,) = _matmul(u, w_in_b, 0, 3 * rw, [(w_out, 0)])
    for l in range(depth):
        first, last = l == 0, l + 1 == depth
        ret, cast = _retention(h_qkvg, tables, batch, seq, [(w_ff1, 0)] if first else [])
        if first:
            (w_ff1_b,) = cast
        pool = _pool_branch(u, w_in_b, w_pool_b, pool_scale, l, seq)
        y = _matmul_cat(ret, pool, w_out_b, 0)
        casts = [] if last else [(w, l + 1) for w in (w_in, w_out, w_ff1, w_ff2)]
        (xf, hdn), cast = _norm_matmul(xf, y, mod(l, 2), ln_mix_g[l], ln_mix_b[l], mod(l, 4), mod(l, 3), w_ff1_b,
                                       w_ff1.shape[2], seq, alpha, casts, relu2=True, emit_u=False)
        y = _matmul_acc(hdn, w_ff2_b, 0)
        if last:
            xf = _resid_norm(xf, y, mod(l, 5), ln_ff_g[l], ln_ff_b[l], seq, alpha)
        else:
            w_in_b, w_out_b, w_ff1_b, w_ff2_b = cast
            (xf, h_qkvg, u), _ = _norm_matmul(xf, y, mod(l, 5), ln_ff_g[l], ln_ff_b[l], mod(l + 1, 1),
                                              mod(l + 1, 0), w_in_b, 3 * rw, seq, alpha, relu2=False, emit_u=True)
    return xf.reshape(batch, seq, d)
```

```python
import functools

import jax
import jax.numpy as jnp
from jax import lax
from jax.experimental import pallas as pl
from jax.experimental.pallas import tpu as pltpu

N_MOD = 6
QK_DIM = 128
V_DIM = 256
CHUNK = 128
POOL_WINDOWS = (2, 4, 8, 16)
POOL_HALO = 16
ROPE_BASE = 10000.0
LN_EPS = 1e-5
SUBLANES = 8
V7X_VMEM_CAP = 60000 * 1024
V7X_MXU_WIDTH = 256

F32 = jnp.float32
BF16 = jnp.bfloat16


def _tile(dim, pref):
    t = min(dim, pref)
    while dim % t:
        t //= 2
    return t


def _params(semantics, vmem_bytes):
    return pltpu.CompilerParams(dimension_semantics=semantics,
                                vmem_limit_bytes=int(min(vmem_bytes, V7X_VMEM_CAP)))


def _ln(x):
    mu = jnp.mean(x, axis=-1, keepdims=True)
    xc = x - mu
    var = jnp.mean(xc * xc, axis=-1, keepdims=True)
    return xc * lax.rsqrt(var + LN_EPS)


def _modvec_kernel(c_ref, w_ref, tab_ref, o_ref):
    s = jax.nn.silu(c_ref[...])
    base = jnp.dot(s.astype(BF16), w_ref[...].astype(BF16), preferred_element_type=F32)
    o_ref[...] = base[None, :, :] + tab_ref[...]


def _modvec(c, w_mod, mod_table):
    b, d = c.shape
    depth = mod_table.shape[0]
    n = w_mod.shape[1]
    tn = _tile(n, 1024)
    c8 = jnp.zeros((SUBLANES, d), F32).at[:b].set(c)
    tab = mod_table.reshape(depth, 1, n)
    return pl.pallas_call(
        _modvec_kernel,
        grid=(n // tn,),
        in_specs=[pl.BlockSpec((SUBLANES, d), lambda j: (0, 0)),
                  pl.BlockSpec((d, tn), lambda j: (0, j)),
                  pl.BlockSpec((depth, 1, tn), lambda j: (0, 0, j))],
        out_specs=pl.BlockSpec((depth, SUBLANES, tn), lambda j: (0, 0, j)),
        out_shape=jax.ShapeDtypeStruct((depth, SUBLANES, n), F32),
        compiler_params=_params(("arbitrary",), 2 * d * tn * 4 + d * tn * 2 + (8 << 20)),
        name="modvec",
    )(c8, w_mod, tab)


def _rope_kernel(pos_ref, freq_ref, sign_ref, cq_ref, sq_ref, ck_ref, sk_ref):
    ang = pos_ref[...].astype(F32) * freq_ref[...]
    cos = jnp.cos(ang)
    sin = jnp.sin(ang) * sign_ref[...]
    scale = QK_DIM ** -0.5
    cq_ref[...] = cos
    sq_ref[...] = sin
    ck_ref[...] = cos * scale
    sk_ref[...] = sin * scale


def _rope_tables(positions):
    m = positions.size
    half = QK_DIM // 2
    inv_freq = ROPE_BASE ** (-jnp.arange(0, QK_DIM, 2, dtype=F32) / QK_DIM)
    freq = jnp.concatenate([inv_freq, inv_freq]).reshape(1, QK_DIM)
    sign = jnp.concatenate([-jnp.ones((half,), F32), jnp.ones((half,), F32)]).reshape(1, QK_DIM)
    tr = _tile(m, 1024)
    vec = pl.BlockSpec((1, QK_DIM), lambda i: (0, 0))
    tab = pl.BlockSpec((tr, QK_DIM), lambda i: (i, 0))
    return pl.pallas_call(
        _rope_kernel,
        grid=(m // tr,),
        in_specs=[pl.BlockSpec((tr, 1), lambda i: (i, 0)), vec, vec],
        out_specs=[tab] * 4,
        out_shape=[jax.ShapeDtypeStruct((m, QK_DIM), F32)] * 4,
        compiler_params=_params(("arbitrary",), 32 << 20),
        name="rope_tables",
    )(positions.reshape(m, 1), freq, sign)


def _ln_mod_kernel(x_ref, sc_ref, sh_ref, u_ref):
    u_ref[...] = (_ln(x_ref[...]) * (1.0 + sc_ref[...]) + sh_ref[...]).astype(u_ref.dtype)


def _ln_mod(x, sc, sh, seq):
    m, d = x.shape
    tm = _tile(seq, 512)
    per_seq = seq // tm
    vec = pl.BlockSpec((None, 1, d), lambda i: (i // per_seq, 0, 0))
    return pl.pallas_call(
        _ln_mod_kernel,
        grid=(m // tm,),
        in_specs=[pl.BlockSpec((tm, d), lambda i: (i, 0)), vec, vec],
        out_specs=pl.BlockSpec((tm, d), lambda i: (i, 0)),
        out_shape=jax.ShapeDtypeStruct((m, d), BF16),
        compiler_params=_params(("arbitrary",), 4 * tm * d * 4 + (8 << 20)),
        name="ln_mod",
    )(x, sc, sh)


def _cast_plan(casts, cast_layer, steps, step_index):
    slabs = 1 << (steps.bit_length() - 1)
    slab = lambda *ids: jnp.minimum(step_index(*ids), slabs - 1)
    in_specs, out_specs, out_shape, vmem = [], [], [], 0
    for wf in casts:
        _, r, c = wf.shape
        rows = r // slabs
        assert rows * slabs == r and rows % 16 == 0, (wf.shape, slabs)
        in_specs.append(pl.BlockSpec((None, rows, c), lambda *ids: (cast_layer, slab(*ids), 0)))
        out_specs.append(pl.BlockSpec((None, rows, c), lambda *ids: (0, slab(*ids), 0)))
        out_shape.append(jax.ShapeDtypeStruct((1, r, c), BF16))
        vmem += 2 * rows * c * (4 + 2)
    return in_specs, out_specs, out_shape, vmem


def _cast_slabs(srcs, dsts):
    for src, dst in zip(srcs, dsts):
        dst[...] = src[...].astype(dst.dtype)


def _mm_kernel(a_ref, b_ref, *refs, n_cast):
    o_ref = refs[n_cast]
    o_ref[...] = jnp.dot(a_ref[...], b_ref[...], preferred_element_type=F32).astype(o_ref.dtype)
    _cast_slabs(refs[:n_cast], refs[n_cast + 1:])


def _matmul(a, w, layer, n, casts=(), cast_layer=None, *, out_dtype=BF16):
    m, k = a.shape
    tm = _tile(m, 1024)
    tn = _tile(n, 1024)
    nj = n // tn
    out_b = jnp.dtype(out_dtype).itemsize
    cast_in, cast_out, cast_shape, cast_vmem = _cast_plan(casts, cast_layer, (m // tm) * nj,
                                                          lambda i, j: i * nj + j)
    vmem = 2 * (tm * k * 2 + k * tn * 2 + tm * tn * out_b) + 2 * tm * tn * 4 + (4 << 20) + cast_vmem
    outs = pl.pallas_call(
        functools.partial(_mm_kernel, n_cast=len(casts)),
        grid=(m // tm, nj),
        in_specs=[pl.BlockSpec((tm, k), lambda i, j: (i, 0)),
                  pl.BlockSpec((None, k, tn), lambda i, j: (layer, 0, j))] + cast_in,
        out_specs=[pl.BlockSpec((tm, tn), lambda i, j: (i, j))] + cast_out,
        out_shape=[jax.ShapeDtypeStruct((m, n), out_dtype)] + cast_shape,
        compiler_params=_params(("arbitrary", "arbitrary"), vmem),
        name="matmul",
    )(a, w, *casts)
    return outs[0], outs[1:]


def _resid_math(x, y, gate, gain, bias, alpha):
    return _ln(alpha * x + gate * y.astype(F32)) * gain + bias


def _norm_mm_kernel(x_ref, y_ref, gate_ref, gain_ref, bias_ref, sc_ref, sh_ref, w_ref, *refs,
                    alpha, rows, relu2, emit_u, n_cast):
    xo_ref, o_ref = refs[n_cast], refs[n_cast + 1]
    uo_ref = refs[n_cast + 2] if emit_u else None
    u0, u1 = refs[-2:]
    i = pl.program_id(0)
    j = pl.program_id(1)

    def norm_rows(u_next):
        xn = _resid_math(x_ref[...], y_ref[...], gate_ref[...], gain_ref[...], bias_ref[...], alpha)
        xo_ref[...] = xn
        u = (_ln(xn) * (1.0 + sc_ref[...]) + sh_ref[...]).astype(u_next.dtype)
        u_next[pl.ds(pl.multiple_of(j * rows, rows), rows), :] = u
        if emit_u:
            uo_ref[...] = u
        _cast_slabs(refs[:n_cast], refs[-2 - n_cast:-2])

    @pl.when(i == 0)
    def _():
        norm_rows(u0)

    for parity, (u_cur, u_next) in enumerate(((u1, u0), (u0, u1))):
        @pl.when((i > 0) & (i % 2 == parity))
        def _():
            acc = jnp.dot(u_cur[...], w_ref[...], preferred_element_type=F32)
            if relu2:
                acc = jnp.square(jnp.maximum(acc, 0.0))
            o_ref[...] = acc.astype(o_ref.dtype)
            norm_rows(u_next)


def _norm_matmul(x, y, gate, gain, bias, sc, sh, w, n, seq, alpha, casts=(), cast_layer=None, *, relu2, emit_u):
    m, d = x.shape
    tm = _tile(seq, 1024)
    nm = m // tm
    tn = next(t for t in range(1024, 0, -V7X_MXU_WIDTH)
              if n % t == 0 and tm % (n // t) == 0 and (tm // (n // t)) % 16 == 0)
    nj = n // tn
    rows = tm // nj
    per_seq = seq // tm
    slab = pl.BlockSpec((rows, d), lambda i, j: (jnp.where(i < nm, i * nj + j, nm * nj - 1), 0))
    per_batch = pl.BlockSpec((None, 1, d), lambda i, j: (jnp.minimum(i, nm - 1) // per_seq, 0, 0))
    shared = pl.BlockSpec((1, d), lambda i, j: (0, 0))
    col = lambda i, j: jnp.where(i > 0, j, 0)
    cast_in, cast_out, cast_shape, cast_vmem = _cast_plan(casts, cast_layer, (nm + 1) * nj,
                                                          lambda i, j: i * nj + j)
    vmem = (2 * tm * d * 2 + 2 * d * tn * 2 + 2 * tm * tn * 2 + 3 * tm * tn * 4
            + 2 * rows * d * (4 + 2 + 4 + 2) + 8 * rows * d * 4 + (4 << 20) + cast_vmem)
    out_specs = [slab, pl.BlockSpec((tm, tn), lambda i, j: (jnp.maximum(i - 1, 0), col(i, j)))]
    out_shape = [jax.ShapeDtypeStruct((m, d), F32), jax.ShapeDtypeStruct((m, n), BF16)]
    if emit_u:
        out_specs.append(slab)
        out_shape.append(jax.ShapeDtypeStruct((m, d), BF16))
    n_main = len(out_shape)
    outs = pl.pallas_call(
        functools.partial(_norm_mm_kernel, alpha=alpha, rows=rows, relu2=relu2, emit_u=emit_u,
                          n_cast=len(casts)),
        grid=(nm + 1, nj),
        in_specs=[slab, slab, per_batch, shared, shared, per_batch, per_batch,
                  pl.BlockSpec((None, d, tn), lambda i, j: (0, 0, col(i, j)))] + cast_in,
        out_specs=out_specs + cast_out,
        out_shape=out_shape + cast_shape,
        scratch_shapes=[pltpu.VMEM((tm, d), BF16), pltpu.VMEM((tm, d), BF16)],
        compiler_params=_params(("arbitrary", "arbitrary"), vmem),
        name="norm_matmul_relu2" if relu2 else "norm_matmul",
    )(x, y, gate, gain.reshape(1, d), bias.reshape(1, d), sc, sh, w, *casts)
    return outs[:n_main], outs[n_main:]


def _mm_acc_kernel(a_ref, b_ref, *refs, n_cast):
    o_ref, acc_ref = refs[n_cast], refs[-1]
    kk = pl.program_id(2)

    @pl.when(kk == 0)
    def _():
        acc_ref[...] = jnp.zeros_like(acc_ref)

    acc_ref[...] += jnp.dot(a_ref[...], b_ref[...], preferred_element_type=F32)
    _cast_slabs(refs[:n_cast], refs[n_cast + 1:-1])

    @pl.when(kk == pl.num_programs(2) - 1)
    def _():
        o_ref[...] = acc_ref[...].astype(o_ref.dtype)


def _matmul_acc(a, w, layer, casts=(), cast_layer=None, *, out_dtype=BF16):
    m, k = a.shape
    n = w.shape[2]
    tm = _tile(m, 1024)
    tn = _tile(n, 2048)
    tk = _tile(k, 2048)
    nj, nk = n // tn, k // tk
    out_b = jnp.dtype(out_dtype).itemsize
    cast_in, cast_out, cast_shape, cast_vmem = _cast_plan(casts, cast_layer, (m // tm) * nj * nk,
                                                          lambda i, j, kk: (i * nj + j) * nk + kk)
    vmem = 2 * (tm * tk * 2 + tk * tn * 2 + tm * tn * out_b) + 2 * tm * tn * 4 + (4 << 20) + cast_vmem
    outs = pl.pallas_call(
        functools.partial(_mm_acc_kernel, n_cast=len(casts)),
        grid=(m // tm, nj, nk),
        in_specs=[pl.BlockSpec((tm, tk), lambda i, j, kk: (i, kk)),
                  pl.BlockSpec((None, tk, tn), lambda i, j, kk: (layer, kk, j))] + cast_in,
        out_specs=[pl.BlockSpec((tm, tn), lambda i, j, kk: (i, j))] + cast_out,
        out_shape=[jax.ShapeDtypeStruct((m, n), out_dtype)] + cast_shape,
        scratch_shapes=[pltpu.VMEM((tm, tn), F32)],
        compiler_params=_params(("arbitrary", "arbitrary", "arbitrary"), vmem),
        name="matmul_acc",
    )(a, w, *casts)
    return outs[0], outs[1:]


def _mm2_kernel(a1_ref, a2_ref, b1_ref, b2_ref, o_ref):
    acc = jnp.dot(a1_ref[...], b1_ref[...], preferred_element_type=F32)
    acc = acc + jnp.dot(a2_ref[...], b2_ref[...], preferred_element_type=F32)
    o_ref[...] = acc.astype(o_ref.dtype)


def _matmul_cat(a1, a2, w, layer, *, out_dtype=BF16):
    m, k1 = a1.shape
    k2 = a2.shape[1]
    assert k1 == k2
    n = w.shape[2]
    tm = _tile(m, 1024)
    tn = _tile(n, 1024)
    out_b = jnp.dtype(out_dtype).itemsize
    vmem = 2 * (2 * tm * k1 * 2 + 2 * k1 * tn * 2 + tm * tn * out_b) + 2 * tm * tn * 4 + (4 << 20)
    return pl.pallas_call(
        _mm2_kernel,
        grid=(m // tm, n // tn),
        in_specs=[pl.BlockSpec((tm, k1), lambda i, j: (i, 0)),
                  pl.BlockSpec((tm, k2), lambda i, j: (i, 0)),
                  pl.BlockSpec((None, k1, tn), lambda i, j: (layer, 0, j)),
                  pl.BlockSpec((None, k2, tn), lambda i, j: (layer, 1, j))],
        out_specs=pl.BlockSpec((tm, tn), lambda i, j: (i, j)),
        out_shape=jax.ShapeDtypeStruct((m, n), out_dtype),
        compiler_params=_params(("arbitrary", "arbitrary"), vmem),
        name="matmul_cat",
    )(a1, a2, w, w)


def _fold_kernel(wp_ref, wmix_ref, scale_ref, o_ref):
    o_ref[...] = (jnp.dot(wp_ref[...], wmix_ref[...], preferred_element_type=F32) * scale_ref[...]).astype(o_ref.dtype)


def _fold_pool_weights(w_in, w_pool, pool_scale, layer):
    d = w_in.shape[1]
    groups = len(POOL_WINDOWS)
    gdim = w_pool.shape[-1]
    pw = groups * gdim
    first = w_in.shape[2] // gdim - groups
    return pl.pallas_call(
        _fold_kernel,
        grid=(groups,),
        in_specs=[pl.BlockSpec((None, d, gdim), lambda g: (0, 0, first + g)),
                  pl.BlockSpec((None, None, gdim, gdim), lambda g: (layer, g, 0, 0)),
                  pl.BlockSpec((None, 1, gdim), lambda g: (layer, 0, g))],
        out_specs=pl.BlockSpec((d, gdim), lambda g: (0, g)),
        out_shape=jax.ShapeDtypeStruct((d, pw), BF16),
        compiler_params=_params(("arbitrary",), 32 << 20),
        name="fold_pool_weights",
    )(w_in, w_pool, pool_scale.reshape(pool_scale.shape[0], 1, pw))


def _pool_kernel(u_ref, wf_ref, o_ref, carry, *, per_seq, tm, gdim):
    i = pl.program_id(0)

    @pl.when((i % per_seq) == 0)
    def _():
        carry[...] = jnp.zeros_like(carry)

    t1 = ((i % per_seq) * tm + 1 + lax.broadcasted_iota(jnp.int32, (tm, 1), 0)).astype(F32)
    u = u_ref[...]
    for g, w in reversed(list(enumerate(POOL_WINDOWS))):
        cols = slice(g * gdim, (g + 1) * gdim)
        p = jnp.dot(u, wf_ref[:, cols], preferred_element_type=F32)
        win = jnp.concatenate([carry[:, cols], p], axis=0)
        carry[:, cols] = p[tm - POOL_HALO:, :]
        span = 1
        while span < w:
            win = win + pltpu.roll(win, span, 0)
            span *= 2
        o_ref[:, cols] = (win[POOL_HALO:, :] / jnp.minimum(t1, float(w)) - p).astype(o_ref.dtype)


def _pool_branch(u, w_fold, seq):
    m, d = u.shape
    pw = w_fold.shape[1]
    gdim = pw // len(POOL_WINDOWS)
    tm = _tile(seq, 512)
    per_seq = seq // tm
    vmem = (2 * tm * d * 2 + d * pw * 2 + 2 * tm * pw * 2 + POOL_HALO * pw * 4
            + 8 * (tm + POOL_HALO) * gdim * 4 + (4 << 20))
    return pl.pallas_call(
        functools.partial(_pool_kernel, per_seq=per_seq, tm=tm, gdim=gdim),
        grid=(m // tm,),
        in_specs=[pl.BlockSpec((tm, d), lambda i: (i, 0)),
                  pl.BlockSpec((d, pw), lambda i: (0, 0), pipeline_mode=pl.Buffered(1))],
        out_specs=pl.BlockSpec((tm, pw), lambda i: (i, 0)),
        out_shape=jax.ShapeDtypeStruct((m, pw), BF16),
        scratch_shapes=[pltpu.VMEM((POOL_HALO, pw), F32)],
        compiler_params=_params(("arbitrary",), vmem),
        name="pool_branch",
    )(u, w_fold)


def _ret_kernel(q_ref, k_ref, v_ref, g_ref, cq_ref, sq_ref, ck_ref, sk_ref, *refs, heads, n_chunks, n_cast):
    o_ref, state_ref, tab_ref = refs[n_cast], refs[-2], refs[-1]
    c = CHUNK

    @pl.when((pl.program_id(0) == 0) & (pl.program_id(1) == 0))
    def _():
        row = lax.broadcasted_iota(jnp.int32, (c, c), 0)
        col = lax.broadcasted_iota(jnp.int32, (c, c), 1)
        diff = (row - col).astype(F32)
        ridx = row.astype(F32)
        for h in range(heads):
            lg = jnp.log(jnp.full((c, c), 1.0 - 2.0 ** (-5.0 - h), F32))
            tab_ref[h, 0] = jnp.where(diff >= 0, jnp.exp(jnp.maximum(diff, 0.0) * lg), 0.0)
            tab_ref[h, 1] = jnp.exp((c - 1.0 - ridx) * lg)
            tab_ref[h, 2] = jnp.exp((ridx + 1.0) * lg)

    @pl.when(pl.program_id(1) == 0)
    def _():
        state_ref[...] = jnp.zeros_like(state_ref)

    def chunk(n, carry):
        rows = pl.ds(pl.multiple_of(n * c, c), c)
        cq, sq, ck, sk = cq_ref[rows, :], sq_ref[rows, :], ck_ref[rows, :], sk_ref[rows, :]
        for h in range(heads):
            cdecay = jnp.exp(c * jnp.log(jnp.full((1, V_DIM), 1.0 - 2.0 ** (-5.0 - h), F32)))
            qs = slice(h * QK_DIM, (h + 1) * QK_DIM)
            vs = slice(h * V_DIM, (h + 1) * V_DIM)
            q = q_ref[rows, qs].astype(F32)
            k = k_ref[rows, qs].astype(F32)
            qr = q * cq + pltpu.roll(q, QK_DIM // 2, 1) * sq
            kr = k * ck + pltpu.roll(k, QK_DIM // 2, 1) * sk
            v = v_ref[rows, vs]
            scores = lax.dot_general(qr.astype(BF16), kr.astype(BF16), (((1,), (1,)), ((), ())),
                                     preferred_element_type=F32) * tab_ref[h, 0]
            kv = jnp.dot((kr * tab_ref[h, 1]).astype(BF16).T, v, preferred_element_type=F32)
            state = state_ref[h]
            lhs = jnp.concatenate([scores.astype(BF16), (qr * tab_ref[h, 2]).astype(BF16)], axis=1)
            rhs = jnp.concatenate([v, state.astype(BF16)], axis=0)
            out = jnp.dot(lhs, rhs, preferred_element_type=F32)
            state_ref[h] = state * cdecay + kv
            gate = jax.nn.silu(g_ref[rows, vs].astype(F32))
            o_ref[rows, vs] = (gate * _ln(out)).astype(o_ref.dtype)
        return carry

    lax.fori_loop(0, n_chunks, chunk, 0)
    _cast_slabs(refs[:n_cast], refs[n_cast + 1:-2])


def _retention(h_qkvg, tables, batch, seq, casts=(), cast_layer=None):
    m, width = h_qkvg.shape
    rw = width // 3
    heads = rw // V_DIM
    qkw = heads * QK_DIM
    assert 2 * qkw == rw and CHUNK == QK_DIM
    tr = _tile(seq, 4 * CHUNK)
    per_seq = seq // tr
    rowblk = lambda b, s: b * per_seq + s
    tab = pl.BlockSpec((tr, QK_DIM), lambda b, s: (rowblk(b, s), 0))
    cast_in, cast_out, cast_shape, cast_vmem = _cast_plan(casts, cast_layer, batch * per_seq, rowblk)
    vmem = (2 * tr * (2 * qkw * 2 + 3 * rw * 2 + 4 * QK_DIM * 4) + heads * QK_DIM * V_DIM * 4
            + heads * 3 * CHUNK * CHUNK * 4 + (16 << 20) + cast_vmem)
    outs = pl.pallas_call(
        functools.partial(_ret_kernel, heads=heads, n_chunks=tr // CHUNK, n_cast=len(casts)),
        grid=(batch, per_seq),
        in_specs=[pl.BlockSpec((tr, qkw), lambda b, s: (rowblk(b, s), 0)),
                  pl.BlockSpec((tr, qkw), lambda b, s: (rowblk(b, s), 1)),
                  pl.BlockSpec((tr, rw), lambda b, s: (rowblk(b, s), 1)),
                  pl.BlockSpec((tr, rw), lambda b, s: (rowblk(b, s), 2)),
                  tab, tab, tab, tab] + cast_in,
        out_specs=[pl.BlockSpec((tr, rw), lambda b, s: (rowblk(b, s), 0))] + cast_out,
        out_shape=[jax.ShapeDtypeStruct((m, rw), BF16)] + cast_shape,
        scratch_shapes=[pltpu.VMEM((heads, QK_DIM, V_DIM), F32),
                        pltpu.VMEM((heads, 3, CHUNK, CHUNK), F32)],
        compiler_params=_params(("arbitrary", "arbitrary"), vmem),
        name="retention",
    )(h_qkvg, h_qkvg, h_qkvg, h_qkvg, *tables, *casts)
    return outs[0], outs[1:]


def _resid_kernel(x_ref, y_ref, gate_ref, gain_ref, bias_ref, xo_ref, *, alpha):
    xo_ref[...] = _resid_math(x_ref[...], y_ref[...], gate_ref[...], gain_ref[...], bias_ref[...], alpha)


def _resid_norm(x, y, gate, gain, bias, seq, alpha):
    m, d = x.shape
    tm = _tile(seq, 256)
    per_seq = seq // tm
    row = pl.BlockSpec((tm, d), lambda i: (i, 0))
    per_batch = pl.BlockSpec((None, 1, d), lambda i: (i // per_seq, 0, 0))
    shared = pl.BlockSpec((1, d), lambda i: (0, 0))
    vmem = 2 * tm * d * (4 + y.dtype.itemsize + 4) + 6 * tm * d * 4 + (4 << 20)
    return pl.pallas_call(
        functools.partial(_resid_kernel, alpha=alpha),
        grid=(m // tm,),
        in_specs=[row, row, per_batch, shared, shared],
        out_specs=row,
        out_shape=jax.ShapeDtypeStruct((m, d), F32),
        compiler_params=_params(("arbitrary",), vmem),
        name="resid_norm",
    )(x, y, gate, gain.reshape(1, d), bias.reshape(1, d))


def kernel(x, c, positions, w_mod, mod_table, w_in, w_pool, pool_scale, w_out, ln_mix_g, ln_mix_b,
           w_ff1, w_ff2, ln_ff_g, ln_ff_b):
    batch, seq, d = x.shape
    depth = w_in.shape[0]
    m = batch * seq
    alpha = (2 * depth) ** 0.25
    rw = d // 2
    assert seq % CHUNK == 0 and rw % V_DIM == 0 and w_in.shape[2] == 4 * rw

    mods = _modvec(c, w_mod, mod_table).reshape(depth, SUBLANES, N_MOD, d)

    def mod(layer, which):
        return mods[layer, :batch, which][:, None, :]

    tables = _rope_tables(positions)
    w_in_b = w_in[:1].astype(BF16)
    w_pool_b = w_pool.astype(BF16)

    xf = x.reshape(m, d)
    u = _ln_mod(xf, mod(0, 1), mod(0, 0), seq)
    h_qkvg, (w_out_b,) = _matmul(u, w_in_b, 0, 3 * rw, [w_out], 0)
    for l in range(depth):
        first, last = l == 0, l + 1 == depth
        ret, cast = _retention(h_qkvg, tables, batch, seq, [w_ff1] if first else [], 0)
        if first:
            (w_ff1_b,) = cast
        pool = _pool_branch(u, _fold_pool_weights(w_in_b, w_pool_b, pool_scale, l), seq)
        y = _matmul_cat(ret, pool, w_out_b, 0)
        (xf, hdn), cast = _norm_matmul(xf, y, mod(l, 2), ln_mix_g[l], ln_mix_b[l], mod(l, 4), mod(l, 3), w_ff1_b,
                                       w_ff1.shape[2], seq, alpha, [w_ff2] if first else [], 0,
                                       relu2=True, emit_u=False)
        if first:
            (w_ff2_b,) = cast
        if last:
            y, _ = _matmul_acc(hdn, w_ff2_b, 0)
            xf = _resid_norm(xf, y, mod(l, 5), ln_ff_g[l], ln_ff_b[l], seq, alpha)
        else:
            y, (w_in_b, w_out_b, w_ff1_b, w_ff2_b) = _matmul_acc(hdn, w_ff2_b, 0, [w_in, w_out, w_ff1, w_ff2], l + 1)
            (xf, h_qkvg, u), _ = _norm_matmul(xf, y, mod(l, 5), ln_ff_g[l], ln_ff_b[l], mod(l + 1, 1),
                                              mod(l + 1, 0), w_in_b, 3 * rw, seq, alpha, relu2=False, emit_u=True)
    return xf.reshape(batch, seq, d)
```
